```python
import math
import jax, jax.numpy as jnp
from jax import lax
import numpy as np

D_MODEL = 2048
BATCH = 16
SEQ = 256
DEPTH = 4
DEC_BATCH = 2
DEC_SEQ = 2048
PAST_LEN = 256

GRID_W = 64
N_HEADS = 8
HEAD_DIM = 128
D_ATT = N_HEADS * HEAD_DIM
D_LRU = D_MODEL // 4
LRU_BLOCKS = 4
LRU_BW = D_LRU // LRU_BLOCKS
LRU_CONV = 4
LRU_C = 8.0
D_SC = D_MODEL // 4
SC_CONV = 3
D_MIX = D_ATT + D_LRU + D_SC
D_IN = 3 * D_ATT + 2 * D_LRU + 3 * D_SC
D_FF = ((-(-8 * D_MODEL // 3)) + 255) // 256 * 256
NA_ROWS = 8
NA_COLS = 16
ATTN_BLOCK = 128
EPS = 1e-6
SPLITS = (D_ATT, 2 * D_ATT, 3 * D_ATT, 3 * D_ATT + D_LRU, 3 * D_ATT + 2 * D_LRU,
          3 * D_ATT + 2 * D_LRU + D_SC, 3 * D_ATT + 2 * D_LRU + 2 * D_SC)

kernel_name = "hybrid_diffusion_prefix_trunk_step"


def rms_norm(x, g):
    xf = x.astype(jnp.float32)
    y = xf * lax.rsqrt(jnp.mean(xf * xf, axis=-1, keepdims=True) + EPS)
    return (y * g.astype(jnp.float32)).astype(x.dtype)


def modulation(cond, w_mod, b_mod):
    m = jnp.dot(jax.nn.silu(cond), w_mod) + b_mod
    return [t[:, None, :] for t in jnp.split(m, 6, axis=-1)]


def dw_conv(x, w, left, right):
    T = x.shape[1]
    xp = jnp.pad(x, ((0, 0), (left, right), (0, 0)))
    out = xp[:, 0:T] * w[0]
    for k in range(1, w.shape[0]):
        out = out + xp[:, k:k + T] * w[k]
    return out


def linear_scan(a, u, h0, reverse):
    if reverse:
        a, u = jnp.flip(a, 1), jnp.flip(u, 1)

    def combine(l, r):
        return l[0] * r[0], r[0] * l[1] + r[1]

    a_cum, b_cum = lax.associative_scan(combine, (a, u), axis=1)
    h = a_cum * h0[:, None, :] + b_cum
    final = h[:, -1]
    if reverse:
        h = jnp.flip(h, 1)
    return h, final


def rglru_mixer(xb, yb, lw, h0):
    bn, t, _ = xb.shape
    xc = dw_conv(xb, lw["lru_conv_w"], LRU_CONV // 2, LRU_CONV - 1 - LRU_CONV // 2) + lw["lru_conv_b"]
    xf = xc.astype(jnp.float32)
    xblk = xf.reshape(bn, t, LRU_BLOCKS, LRU_BW)
    hs, finals = [], []
    for d in range(2):
        r = jax.nn.sigmoid(jnp.einsum('btnk,nkj->btnj', xblk, lw["lru_w_a"][d].astype(jnp.float32)).reshape(bn, t, D_LRU) + lw["lru_b_a"][d].astype(jnp.float32))
        i = jax.nn.sigmoid(jnp.einsum('btnk,nkj->btnj', xblk, lw["lru_w_i"][d].astype(jnp.float32)).reshape(bn, t, D_LRU) + lw["lru_b_i"][d].astype(jnp.float32))
        log_a = -LRU_C * r * jax.nn.softplus(-lw["lru_lambda"][d].astype(jnp.float32))
        a = jnp.exp(log_a)
        u = jnp.sqrt(-jnp.expm1(2.0 * log_a)) * (i * xf)
        h, h_final = linear_scan(a, u, h0[:, d].astype(jnp.float32), reverse=(d == 1))
        hs.append(h)
        finals.append(h_final)
    y = (hs[0] + hs[1]) * jax.nn.gelu(yb.astype(jnp.float32))
    return y.astype(xb.dtype), jnp.stack(finals, axis=1)


def short_conv_mixer(b_gate, c_gate, xs, w):
    return b_gate * dw_conv(c_gate * xs, w, SC_CONV // 2, SC_CONV - 1 - SC_CONV // 2)


def heads(t):
    return t.reshape(t.shape[0], t.shape[1], N_HEADS, HEAD_DIM)


def context_attention(q, k, v):
    bn, length = q.shape[:2]
    scale = HEAD_DIM ** -0.5
    qb = q.reshape(bn, length // ATTN_BLOCK, ATTN_BLOCK, N_HEADS, HEAD_DIM).transpose(1, 0, 2, 3, 4)

    def block(qi):
        s = jnp.einsum('bqhd,bkhd->bhqk', qi, k).astype(jnp.float32) * scale
        p = jax.nn.softmax(s, axis=-1).astype(v.dtype)
        return jnp.einsum('bhqk,bkhd->bqhd', p, v)

    o = lax.map(block, qb)
    return o.transpose(1, 0, 2, 3, 4).reshape(bn, length, N_HEADS, HEAD_DIM)


def neighbourhood_attention(q, k, v, k_ctx, v_ctx, rel_bias):
    bn, n = q.shape[:2]
    rows = n // GRID_W
    kr = min(NA_ROWS, rows)
    scale = HEAD_DIM ** -0.5
    qg = q.reshape(bn, rows, GRID_W, N_HEADS, HEAD_DIM)
    kg = k.reshape(bn, rows, GRID_W, N_HEADS, HEAD_DIM)
    vg = v.reshape(bn, rows, GRID_W, N_HEADS, HEAD_DIM)
    row_start = jnp.clip(jnp.arange(rows) - kr // 2, 0, rows - kr)
    cols = jnp.arange(GRID_W)
    col_start = jnp.clip(cols - NA_COLS // 2, 0, GRID_W - NA_COLS)
    col_idx = col_start[:, None] + jnp.arange(NA_COLS)
    dc_idx = col_idx - cols[:, None] + (NA_COLS - 1)
    n_loc = kr * NA_COLS

    def row_block(r):
        rs = row_start[r]
        q_r = lax.dynamic_index_in_dim(qg, r, axis=1, keepdims=False)
        k_rows = lax.dynamic_slice_in_dim(kg, rs, kr, axis=1)
        v_rows = lax.dynamic_slice_in_dim(vg, rs, kr, axis=1)
        k_win = k_rows[:, :, col_idx]
        v_win = v_rows[:, :, col_idx]
        dr_idx = rs + jnp.arange(kr) - r + (NA_ROWS - 1)
        bias = rel_bias[:, dr_idx][:, :, dc_idx].transpose(0, 2, 1, 3)
        s_loc = jnp.einsum('bqhd,biqjhd->bhqij', q_r, k_win).astype(jnp.float32) * scale + bias.astype(jnp.float32)
        s_loc = s_loc.reshape(bn, N_HEADS, GRID_W, n_loc)
        s_ctx = jnp.einsum('bqhd,bkhd->bhqk', q_r, k_ctx).astype(jnp.float32) * scale
        p = jax.nn.softmax(jnp.concatenate([s_loc, s_ctx], axis=-1), axis=-1).astype(v.dtype)
        p_loc = p[..., :n_loc].reshape(bn, N_HEADS, GRID_W, kr, NA_COLS)
        p_ctx = p[..., n_loc:]
        return (jnp.einsum('bhqij,biqjhd->bqhd', p_loc, v_win)
                + jnp.einsum('bhqk,bkhd->bqhd', p_ctx, v_ctx))

    o = lax.map(row_block, jnp.arange(rows))
    return o.transpose(1, 0, 2, 3, 4).reshape(bn, n, N_HEADS, HEAD_DIM)


def swiglu_ffn(h, g, shift, scale, lw):
    u = rms_norm(h, g) * (1 + scale) + shift
    return jnp.dot(jax.nn.silu(jnp.dot(u, lw["w_ffn_gate"])) * jnp.dot(u, lw["w_ffn_up"]), lw["w_ffn_down"])


def trunk_layer(h, mod, lw, ctx):
    sh1, sc1, g1, sh2, sc2, g2 = mod
    bn, t, _ = h.shape
    u = rms_norm(h, lw["g_mix"]) * (1 + sc1) + sh1
    q, k, v, lx, ly, sb, scg, sx = jnp.split(jnp.dot(u, lw["w_in"]), SPLITS, axis=-1)
    q = rms_norm(heads(q), lw["q_gain"])
    k = rms_norm(heads(k), lw["k_gain"])
    v = heads(v)
    if ctx is None:
        att = context_attention(q, k, v)
        lru, lru_final = rglru_mixer(lx, ly, lw, jnp.zeros((bn, 2, D_LRU), jnp.float32))
    else:
        att = neighbourhood_attention(q, k, v, ctx[0], ctx[1], lw["na_bias"])
        lru, lru_final = rglru_mixer(lx, ly, lw, ctx[2])
    conv = short_conv_mixer(sb, scg, sx, lw["sc_conv_w"])
    mixed = jnp.concatenate([att.reshape(bn, t, D_ATT), lru, conv], axis=-1)
    h = h + g1 * jnp.dot(mixed, lw["w_out"])
    h = h + g2 * swiglu_ffn(h, lw["g_ffn"], sh2, sc2, lw)
    if ctx is None:
        return h, (k, v, lru_final)
    return h, None


def setup_inputs(seed: int = 0) -> dict:
    key = jax.random.key(seed)
    ks = jax.random.split(key, 27)
    nrm = jax.random.normal
    f32 = jnp.float32
    a0 = jax.random.uniform(ks[21], (DEPTH, 2, D_LRU), f32, minval=0.9, maxval=0.999)
    s = a0 ** (1.0 / LRU_C)
    lru_lambda = jnp.log(s) - jnp.log1p(-s)
    return {
        "x_prompt": nrm(ks[0], (BATCH, SEQ, D_MODEL), f32),
        "x_sample": nrm(ks[1], (DEC_BATCH, DEC_SEQ, D_MODEL), f32),
        "cache_k": nrm(ks[2], (DEC_BATCH, DEPTH, PAST_LEN, N_HEADS, HEAD_DIM), f32),
        "cache_v": nrm(ks[3], (DEC_BATCH, DEPTH, PAST_LEN, N_HEADS, HEAD_DIM), f32),
        "state_lru": 0.5 * nrm(ks[4], (DEC_BATCH, DEPTH, 2, D_LRU), f32),
        "c": nrm(ks[5], (DEC_BATCH, D_MODEL), f32),
        "c_ctx": nrm(ks[6], (D_MODEL,), f32),
        "w_mod": 0.3 * D_MODEL ** -0.5 * nrm(ks[7], (DEPTH, D_MODEL, 6 * D_MODEL), f32),
        "b_mod": 0.01 * nrm(ks[8], (DEPTH, 6 * D_MODEL), f32),
        "g_mix": 1.0 + 0.01 * nrm(ks[9], (DEPTH, D_MODEL), f32),
        "g_ffn": 1.0 + 0.01 * nrm(ks[10], (DEPTH, D_MODEL), f32),
        "w_in": D_MODEL ** -0.5 * nrm(ks[11], (DEPTH, D_MODEL, D_IN), f32),
        "q_gain": 1.0 + 0.01 * nrm(ks[12], (DEPTH, HEAD_DIM), f32),
        "k_gain": 1.0 + 0.01 * nrm(ks[13], (DEPTH, HEAD_DIM), f32),
        "na_bias": 0.02 * nrm(ks[14], (DEPTH, N_HEADS, 2 * NA_ROWS - 1, 2 * NA_COLS - 1), f32),
        "lru_conv_w": LRU_CONV ** -0.5 * nrm(ks[15], (DEPTH, LRU_CONV, D_LRU), f32),
        "lru_conv_b": 0.01 * nrm(ks[16], (DEPTH, D_LRU), f32),
        "lru_w_a": LRU_BW ** -0.5 * nrm(ks[17], (DEPTH, 2, LRU_BLOCKS, LRU_BW, LRU_BW), f32),
        "lru_b_a": 0.01 * nrm(ks[18], (DEPTH, 2, D_LRU), f32),
        "lru_w_i": LRU_BW ** -0.5 * nrm(ks[19], (DEPTH, 2, LRU_BLOCKS, LRU_BW, LRU_BW), f32),
        "lru_b_i": 0.01 * nrm(ks[20], (DEPTH, 2, D_LRU), f32),
        "lru_lambda": lru_lambda,
        "sc_conv_w": SC_CONV ** -0.5 * nrm(ks[22], (DEPTH, SC_CONV, D_SC), f32),
        "w_out": D_MIX ** -0.5 * nrm(ks[23], (DEPTH, D_MIX, D_MODEL), f32),
        "w_ffn_gate": D_MODEL ** -0.5 * nrm(ks[24], (DEPTH, D_MODEL, D_FF), f32),
        "w_ffn_up": D_MODEL ** -0.5 * nrm(ks[25], (DEPTH, D_MODEL, D_FF), f32),
        "w_ffn_down": D_FF ** -0.5 * nrm(ks[26], (DEPTH, D_FF, D_MODEL), f32),
    }


def reference(x_prompt, x_sample, cache_k, cache_v, state_lru, c, c_ctx, w_mod, b_mod, g_mix, g_ffn,
              w_in, q_gain, k_gain, na_bias, lru_conv_w, lru_conv_b, lru_w_a, lru_b_a, lru_w_i, lru_b_i,
              lru_lambda, sc_conv_w, w_out, w_ffn_gate, w_ffn_up, w_ffn_down):
    hp = x_prompt
    hs = x_sample
    new_k, new_v, new_s = [], [], []
    for l in range(DEPTH):
        lw = {
            "g_mix": g_mix[l], "g_ffn": g_ffn[l], "w_in": w_in[l], "q_gain": q_gain[l],
            "k_gain": k_gain[l], "na_bias": na_bias[l], "lru_conv_w": lru_conv_w[l],
            "lru_conv_b": lru_conv_b[l], "lru_w_a": lru_w_a[l], "lru_b_a": lru_b_a[l],
            "lru_w_i": lru_w_i[l], "lru_b_i": lru_b_i[l], "lru_lambda": lru_lambda[l],
            "sc_conv_w": sc_conv_w[l], "w_out": w_out[l], "w_ffn_gate": w_ffn_gate[l],
            "w_ffn_up": w_ffn_up[l], "w_ffn_down": w_ffn_down[l],
        }
        mod_ctx = modulation(c_ctx[None, :], w_mod[l], b_mod[l])
        mod_lat = modulation(c, w_mod[l], b_mod[l])
        hp, (k_l, v_l, s_l) = trunk_layer(hp, mod_ctx, lw, None)
        new_k.append(k_l)
        new_v.append(v_l)
        new_s.append(s_l)
        hs, _ = trunk_layer(hs, mod_lat, lw, (cache_k[:, l], cache_v[:, l], state_lru[:, l]))
    new_cache_k = jnp.stack(new_k, axis=1)
    new_cache_v = jnp.stack(new_v, axis=1)
    new_state_lru = jnp.stack(new_s, axis=1)
    return (hp, hs, new_cache_k, new_cache_v, new_state_lru)
```

```python
import functools

import jax
import jax.numpy as jnp
from jax import lax
from jax.experimental import pallas as pl
from jax.experimental.pallas import tpu as pltpu

D_MODEL = 2048
N_HEADS = 8
HEAD_DIM = 128
D_ATT = N_HEADS * HEAD_DIM
D_LRU = 512
LRU_BLOCKS = 4
LRU_BW = 128
LRU_CONV = 4
LRU_C = 8.0
D_SC = 512
SC_CONV = 3
D_IN = 3 * D_ATT + 2 * D_LRU + 3 * D_SC
GRID_W = 64
NA_ROWS = 8
NA_COLS = 16
EPS = 1e-6
ATT_SCALE = HEAD_DIM ** -0.5

NA_QROWS = 4
NA_KROWS = 12
NA_QBLK = NA_QROWS * GRID_W
NA_KBLK = NA_KROWS * GRID_W
MASK_VALUE = -1e30

SUBLANES = 8
MOD_ROWS = 8
VMEM_LIMIT_BYTES = 56 * 1024 * 1024

F32 = jnp.float32
BF16 = jnp.bfloat16


def _params(*semantics):
    return pltpu.CompilerParams(dimension_semantics=semantics,
                                vmem_limit_bytes=VMEM_LIMIT_BYTES)


def _dot(a, b):
    return jnp.dot(a, b, preferred_element_type=F32)


def _dot_nt(a, b):
    return lax.dot_general(a, b, (((1,), (1,)), ((), ())), preferred_element_type=F32)


def _sigmoid(x):
    return 1.0 / (1.0 + jnp.exp(-x))


def _head_rms(x, gain):
    return x * lax.rsqrt(jnp.mean(x * x, axis=-1, keepdims=True) + EPS) * gain


def _mod_kernel(c_ref, w_ref, b_ref, o_ref):
    c = c_ref[...]
    s = (c * _sigmoid(c)).astype(BF16)
    o_ref[0] = _dot(s, w_ref[0].astype(BF16)) + b_ref[0]


def _modulation(cond, w_mod, b_mod, *, tn=1024):
    depth, d, n = w_mod.shape
    return pl.pallas_call(
        _mod_kernel,
        grid=(depth, n // tn),
        in_specs=[
            pl.BlockSpec((MOD_ROWS, d), lambda l, j: (0, 0)),
            pl.BlockSpec((1, d, tn), lambda l, j: (l, 0, j)),
            pl.BlockSpec((1, 1, tn), lambda l, j: (l, 0, j)),
        ],
        out_specs=pl.BlockSpec((1, MOD_ROWS, tn), lambda l, j: (l, 0, j)),
        out_shape=jax.ShapeDtypeStruct((depth, MOD_ROWS, n), F32),
        compiler_params=_params("arbitrary", "arbitrary"),
        name="modulation",
    )(cond, w_mod, b_mod.reshape(depth, 1, n))


def _norm_mm_kernel(x_ref, mod_ref, g_ref, *rest, n_w, shift_idx, scale_idx, rows):
    w_refs, o_ref, u_ref = rest[:n_w], rest[n_w], rest[n_w + 1]

    @pl.when(pl.program_id(1) == 0)
    def _():
        g = g_ref[...]
        sc = 1.0 + mod_ref[0, scale_idx:scale_idx + 1, :]
        sh = mod_ref[0, shift_idx:shift_idx + 1, :]

        def chunk(r, carry):
            sl = pl.ds(pl.multiple_of(r * rows, rows), rows)
            x = x_ref[sl, :]
            y = x * lax.rsqrt(jnp.mean(x * x, axis=-1, keepdims=True) + EPS) * g
            u_ref[sl, :] = (y * sc + sh).astype(BF16)
            return carry

        lax.fori_loop(0, x_ref.shape[0] // rows, chunk, 0)

    u = u_ref[...]
    if n_w == 1:
        o_ref[...] = _dot(u, w_refs[0][...].astype(BF16)).astype(o_ref.dtype)
    else:
        a = _dot(u, w_refs[0][...].astype(BF16))
        b = _dot(u, w_refs[1][...].astype(BF16))
        o_ref[...] = (a * _sigmoid(a) * b).astype(o_ref.dtype)


def _norm_matmul(x, mod, gain, weights, *, shift_idx, scale_idx, rows_per_mod, out_dtype,
                 tm, tn, name):
    m, d = x.shape
    n = weights[0].shape[1]
    tiles_per_mod = rows_per_mod // tm
    kernel = functools.partial(_norm_mm_kernel, n_w=len(weights), shift_idx=shift_idx,
                               scale_idx=scale_idx, rows=128)
    return pl.pallas_call(
        kernel,
        grid=(m // tm, n // tn),
        in_specs=[
            pl.BlockSpec((tm, d), lambda i, j: (i, 0)),
            pl.BlockSpec((1, 6, d), lambda i, j: (i // tiles_per_mod, 0, 0)),
            pl.BlockSpec((1, d), lambda i, j: (0, 0)),
        ] + [pl.BlockSpec((d, tn), lambda i, j: (0, j)) for _ in weights],
        out_specs=pl.BlockSpec((tm, tn), lambda i, j: (i, j)),
        out_shape=jax.ShapeDtypeStruct((m, n), out_dtype),
        scratch_shapes=[pltpu.VMEM((tm, d), BF16)],
        compiler_params=_params("arbitrary", "arbitrary"),
        name=name,
    )(x, mod, gain.reshape(1, d), *weights)


def _mm_res_kernel(*refs, splits, gate_idx):
    n = len(splits)
    a_refs = refs[:n]
    w_ref, h_ref, mod_ref, o_ref = refs[n:]
    acc = None
    k0 = 0
    for a_ref, kk in zip(a_refs, splits):
        part = _dot(a_ref[...], w_ref[k0:k0 + kk, :].astype(BF16))
        acc = part if acc is None else acc + part
        k0 += kk
    o_ref[...] = h_ref[...] + mod_ref[0, gate_idx:gate_idx + 1, :] * acc


def _matmul_residual(a_parts, w, h, mod, *, gate_idx, rows_per_mod, tm, tn, name):
    m, n = h.shape
    k = w.shape[0]
    splits = tuple(a.shape[1] for a in a_parts)
    tiles_per_mod = rows_per_mod // tm
    kernel = functools.partial(_mm_res_kernel, splits=splits, gate_idx=gate_idx)
    return pl.pallas_call(
        kernel,
        grid=(m // tm, n // tn),
        in_specs=[pl.BlockSpec((tm, kk), lambda i, j: (i, 0)) for kk in splits] + [
            pl.BlockSpec((k, tn), lambda i, j: (0, j)),
            pl.BlockSpec((tm, tn), lambda i, j: (i, j)),
            pl.BlockSpec((1, 6, tn), lambda i, j: (i // tiles_per_mod, 0, j)),
        ],
        out_specs=pl.BlockSpec((tm, tn), lambda i, j: (i, j)),
        out_shape=jax.ShapeDtypeStruct((m, n), F32),
        compiler_params=_params("arbitrary", "arbitrary"),
        name=name,
    )(*a_parts, w, h, mod)


def _ctx_attn_kernel(q_ref, k_ref, v_ref, qg_ref, kg_ref, att_ref, kout_ref):
    qg = qg_ref[...]
    kg = kg_ref[...]
    for h in range(N_HEADS):
        sl = slice(h * HEAD_DIM, (h + 1) * HEAD_DIM)
        qn = _head_rms(q_ref[:, sl], qg)
        kn = _head_rms(k_ref[:, sl], kg)
        kout_ref[:, sl] = kn
        s = _dot_nt(qn.astype(BF16), kn.astype(BF16)) * ATT_SCALE
        e = jnp.exp(s - jnp.max(s, axis=-1, keepdims=True))
        den = jnp.sum(e, axis=-1, keepdims=True)
        o = _dot(e.astype(BF16), v_ref[:, sl].astype(BF16)) / den
        att_ref[:, sl] = o.astype(BF16)


def _context_attention(proj, q_gain, k_gain, *, seq):
    m = proj.shape[0]
    blk = lambda c: pl.BlockSpec((seq, D_ATT), lambda b, c=c: (b, c))
    gain = pl.BlockSpec((1, HEAD_DIM), lambda b: (0, 0))
    return pl.pallas_call(
        _ctx_attn_kernel,
        grid=(m // seq,),
        in_specs=[blk(0), blk(1), blk(2), gain, gain],
        out_specs=[pl.BlockSpec((seq, D_ATT), lambda b: (b, 0)),
                   pl.BlockSpec((seq, D_ATT), lambda b: (b, 0))],
        out_shape=[jax.ShapeDtypeStruct((m, D_ATT), BF16),
                   jax.ShapeDtypeStruct((m, D_ATT), F32)],
        compiler_params=_params("arbitrary"),
        name="context_attention",
    )(proj, proj, proj, q_gain.reshape(1, HEAD_DIM), k_gain.reshape(1, HEAD_DIM))


def _qk_norm_kernel(q_ref, k_ref, v_ref, qg_ref, kg_ref, qo_ref, ko_ref, vo_ref):
    qg = qg_ref[...]
    kg = kg_ref[...]
    for h in range(N_HEADS):
        sl = slice(h * HEAD_DIM, (h + 1) * HEAD_DIM)
        qo_ref[:, sl] = _head_rms(q_ref[:, sl], qg).astype(BF16)
        ko_ref[:, sl] = _head_rms(k_ref[:, sl], kg).astype(BF16)
    vo_ref[...] = v_ref[...].astype(BF16)


def _qk_norm(proj, q_gain, k_gain, *, tm=256):
    m = proj.shape[0]
    blk = lambda c: pl.BlockSpec((tm, D_ATT), lambda i, c=c: (i, c))
    gain = pl.BlockSpec((1, HEAD_DIM), lambda i: (0, 0))
    out = pl.BlockSpec((tm, D_ATT), lambda i: (i, 0))
    return pl.pallas_call(
        _qk_norm_kernel,
        grid=(m // tm,),
        in_specs=[blk(0), blk(1), blk(2), gain, gain],
        out_specs=[out, out, out],
        out_shape=[jax.ShapeDtypeStruct((m, D_ATT), BF16)] * 3,
        compiler_params=_params("arbitrary"),
        name="qk_norm",
    )(proj, proj, proj, q_gain.reshape(1, HEAD_DIM), k_gain.reshape(1, HEAD_DIM))


def _na_block_geometry():
    rows = 32
    return ((0, 0), (NA_QROWS, 0), (rows - NA_QROWS, rows - NA_KROWS)), rows


def _na_bias_kernel(rel_ref, o_ref):
    qc = lax.broadcasted_iota(jnp.int32, (GRID_W, 128), 0)
    kc = lax.broadcasted_iota(jnp.int32, (GRID_W, 128), 1)
    cs = jnp.clip(qc - NA_COLS // 2, 0, GRID_W - NA_COLS)
    col_ok = (kc >= cs) & (kc < cs + NA_COLS)
    tiles = []
    for dr in range(2 * NA_ROWS - 1):
        x = jnp.broadcast_to(rel_ref[0, 0, dr:dr + 1, :], (GRID_W, 128))
        for bit in range(6):
            x = jnp.where(((qc >> bit) & 1) == 1, pltpu.roll(x, 1 << bit, axis=1), x)
        tiles.append(jnp.where(col_ok, x, MASK_VALUE)[:, :GRID_W])
    masked = jnp.full((GRID_W, GRID_W), MASK_VALUE, F32)
    kinds, rows = _na_block_geometry()
    for kind, (r0, kb) in enumerate(kinds):
        for qr in range(NA_QROWS):
            r = r0 + qr
            rs = min(max(r - NA_ROWS // 2, 0), rows - NA_ROWS)
            for kr in range(NA_KROWS):
                krow = kb + kr
                tile = tiles[krow - r + NA_ROWS - 1] if rs <= krow < rs + NA_ROWS else masked
                o_ref[0, kind, 0, qr * GRID_W:(qr + 1) * GRID_W,
                      kr * GRID_W:(kr + 1) * GRID_W] = tile


def _na_bias(na_bias):
    depth = na_bias.shape[0]
    n_dr = 2 * NA_ROWS - 1
    rel = jnp.pad(na_bias, ((0, 0), (0, 0), (0, 0), (0, 128 - (2 * NA_COLS - 1))))
    rel = jnp.roll(rel, -(NA_COLS - 1), axis=-1)
    return pl.pallas_call(
        _na_bias_kernel,
        grid=(depth, N_HEADS),
        in_specs=[pl.BlockSpec((1, 1, n_dr, 128), lambda l, h: (l, h, 0, 0))],
        out_specs=pl.BlockSpec((1, 3, 1, NA_QBLK, NA_KBLK), lambda l, h: (l, 0, h, 0, 0)),
        out_shape=jax.ShapeDtypeStruct((depth, 3, N_HEADS, NA_QBLK, NA_KBLK), F32),
        compiler_params=_params("arbitrary", "arbitrary"),
        name="na_bias",
    )(rel)


def _na_attn_kernel(q_ref, k_ref, v_ref, kc_ref, vc_ref, bias_ref, o_ref, *, n_blocks):
    blk = pl.program_id(1)
    start = pl.multiple_of(jnp.clip(blk - 1, 0, n_blocks - 3) * NA_QBLK, NA_QBLK)
    for h in range(N_HEADS):
        sl = slice(h * HEAD_DIM, (h + 1) * HEAD_DIM)
        q = q_ref[:, sl]
        kw = k_ref[pl.ds(start, NA_KBLK), sl]
        vw = v_ref[pl.ds(start, NA_KBLK), sl]
        s_loc = _dot_nt(q, kw) * ATT_SCALE + bias_ref[0, h]
        s_ctx = _dot_nt(q, kc_ref[0, :, sl].astype(BF16)) * ATT_SCALE
        mx = jnp.maximum(jnp.max(s_loc, axis=-1, keepdims=True),
                         jnp.max(s_ctx, axis=-1, keepdims=True))
        e_loc = jnp.exp(s_loc - mx)
        e_ctx = jnp.exp(s_ctx - mx)
        den = jnp.sum(e_loc, axis=-1, keepdims=True) + jnp.sum(e_ctx, axis=-1, keepdims=True)
        o = _dot(e_loc.astype(BF16), vw) + _dot(e_ctx.astype(BF16), vc_ref[0, :, sl].astype(BF16))
        o_ref[:, sl] = (o / den).astype(BF16)


def _neighbourhood_attention(qn, kn, vb, k_ctx, v_ctx, bias, *, seq):
    m = qn.shape[0]
    batch = m // seq
    n_blocks = seq // NA_QBLK
    past = k_ctx.shape[1]
    kind = lambda i: jnp.where(i == 0, 0, jnp.where(i == n_blocks - 1, 2, 1))
    whole = pl.BlockSpec((seq, D_ATT), lambda b, i: (b, 0))
    ctx = pl.BlockSpec((1, past, D_ATT), lambda b, i: (b, 0, 0))
    return pl.pallas_call(
        functools.partial(_na_attn_kernel, n_blocks=n_blocks),
        grid=(batch, n_blocks),
        in_specs=[
            pl.BlockSpec((NA_QBLK, D_ATT), lambda b, i: (b * n_blocks + i, 0)),
            whole, whole, ctx, ctx,
            pl.BlockSpec((1, N_HEADS, NA_QBLK, NA_KBLK), lambda b, i: (kind(i), 0, 0, 0)),
        ],
        out_specs=pl.BlockSpec((NA_QBLK, D_ATT), lambda b, i: (b * n_blocks + i, 0)),
        out_shape=jax.ShapeDtypeStruct((m, D_ATT), BF16),
        compiler_params=_params("arbitrary", "arbitrary"),
        name="neighbourhood_attention",
    )(qn, kn, vb, k_ctx, v_ctx, bias)


PAD = SUBLANES


def _gelu_tanh(x):
    return x * (0.5 * (1.0 + jnp.tanh(0.7978845608028654 * (x + 0.044715 * (x * x * x)))))


def _lru_kernel(lx_ref, ly_ref, h0_ref, cw_ref, cb_ref, w_ref, b_ref, lam_ref,
                y_ref, fin_ref, xpad, a_buf, u_buf, *, seq, rows):
    zeros = jnp.zeros((PAD, D_LRU), F32)
    xpad[0:PAD, :] = zeros
    xpad[PAD + seq:2 * PAD + seq, :] = zeros
    xpad[PAD:PAD + seq, :] = lx_ref[...]
    cw = cw_ref[...]
    neg_lam = -lam_ref[...]
    softplus = jnp.maximum(neg_lam, 0.0) + jnp.log1p(jnp.exp(-jnp.abs(neg_lam)))
    left = LRU_CONV // 2
    for c in range(seq // rows):
        base = c * rows
        xc = cb_ref[...]
        for k in range(LRU_CONV):
            off = PAD + base + k - left
            xc = xc + xpad[off:off + rows, :] * cw[k:k + 1, :]
        for n in range(LRU_BLOCKS):
            cols = slice(n * LRU_BW, (n + 1) * LRU_BW)
            xb = xc[:, cols]
            g = _dot(xb.astype(BF16), w_ref[n].astype(BF16)) + b_ref[n:n + 1, :]
            for d in range(2):
                r = _sigmoid(g[:, (2 * d) * LRU_BW:(2 * d + 1) * LRU_BW])
                i = _sigmoid(g[:, (2 * d + 1) * LRU_BW:(2 * d + 2) * LRU_BW])
                log_a = -LRU_C * r * softplus[d:d + 1, cols]
                a_buf[d, base:base + rows, cols] = jnp.exp(log_a)
                th = jnp.tanh(log_a)
                u_buf[d, base:base + rows, cols] = jnp.sqrt(-2.0 * th / (1.0 - th)) * (i * xb)

    def step(t, carry):
        hf, hb = carry
        tf = pl.ds(t, 1)
        hf = a_buf[0, tf, :] * hf + u_buf[0, tf, :]
        u_buf[0, tf, :] = hf
        tb = pl.ds(seq - 1 - t, 1)
        hb = a_buf[1, tb, :] * hb + u_buf[1, tb, :]
        u_buf[1, tb, :] = hb
        return hf, hb

    hf, hb = lax.fori_loop(0, seq, step, (h0_ref[0, 0:1, :], h0_ref[0, 1:2, :]), unroll=8)
    fin_ref[0, 0:1, :] = hf
    fin_ref[0, 1:2, :] = hb
    for c in range(seq // rows):
        sl = slice(c * rows, (c + 1) * rows)
        y_ref[sl, :] = ((u_buf[0, sl, :] + u_buf[1, sl, :]) * _gelu_tanh(ly_ref[sl, :])).astype(BF16)


def _rglru(proj, h0, conv_w, conv_b, w_cat, b_cat, lam, *, seq):
    m = proj.shape[0]
    batch = m // seq
    lx_col = 3 * D_ATT // D_LRU
    const = lambda shape: pl.BlockSpec(shape, lambda b: (0,) * len(shape))
    return pl.pallas_call(
        functools.partial(_lru_kernel, seq=seq, rows=256),
        grid=(batch,),
        in_specs=[
            pl.BlockSpec((seq, D_LRU), lambda b: (b, lx_col)),
            pl.BlockSpec((seq, D_LRU), lambda b: (b, lx_col + 1)),
            pl.BlockSpec((1, 2, D_LRU), lambda b: (b, 0, 0)),
            const((LRU_CONV, D_LRU)), const((1, D_LRU)),
            const((LRU_BLOCKS, LRU_BW, 4 * LRU_BW)), const((LRU_BLOCKS, 4 * LRU_BW)),
            const((2, D_LRU)),
        ],
        out_specs=[pl.BlockSpec((seq, D_LRU), lambda b: (b, 0)),
                   pl.BlockSpec((1, 2, D_LRU), lambda b: (b, 0, 0))],
        out_shape=[jax.ShapeDtypeStruct((m, D_LRU), BF16),
                   jax.ShapeDtypeStruct((batch, 2, D_LRU), F32)],
        scratch_shapes=[pltpu.VMEM((seq + 2 * PAD, D_LRU), F32),
                        pltpu.VMEM((2, seq, D_LRU), F32),
                        pltpu.VMEM((2, seq, D_LRU), F32)],
        compiler_params=_params("arbitrary"),
        name="rglru",
    )(proj, proj, h0, conv_w, conv_b.reshape(1, D_LRU), w_cat, b_cat, lam)


def _sconv_kernel(b_ref, c_ref, x_ref, w_ref, y_ref, pad, *, seq):
    zeros = jnp.zeros((PAD, LRU_BW), F32)
    pad[0:PAD, :] = zeros
    pad[PAD + seq:2 * PAD + seq, :] = zeros
    pad[PAD:PAD + seq, :] = c_ref[...] * x_ref[...]
    w = w_ref[...]
    left = SC_CONV // 2
    acc = None
    for k in range(SC_CONV):
        off = PAD + k - left
        term = pad[off:off + seq, :] * w[k:k + 1, :]
        acc = term if acc is None else acc + term
    y_ref[...] = (b_ref[...] * acc).astype(BF16)


def _short_conv(proj, w, *, seq):
    m = proj.shape[0]
    tc = LRU_BW
    col0 = (3 * D_ATT + 2 * D_LRU) // tc
    nch = D_SC // tc
    blk = lambda part: pl.BlockSpec((seq, tc), lambda b, c, part=part: (b, col0 + part * nch + c))
    return pl.pallas_call(
        functools.partial(_sconv_kernel, seq=seq),
        grid=(m // seq, nch),
        in_specs=[blk(0), blk(1), blk(2), pl.BlockSpec((SC_CONV, tc), lambda b, c: (0, c))],
        out_specs=pl.BlockSpec((seq, tc), lambda b, c: (b, c)),
        out_shape=jax.ShapeDtypeStruct((m, D_SC), BF16),
        scratch_shapes=[pltpu.VMEM((seq + 2 * PAD, tc), F32)],
        compiler_params=_params("arbitrary", "arbitrary"),
        name="short_conv",
    )(proj, proj, proj, w)


def _lru_gate_weights(w_a, b_a, w_i, b_i):
    w_cat = jnp.concatenate([w_a[0], w_i[0], w_a[1], w_i[1]], axis=-1)
    blocks = lambda b: b.reshape(2, LRU_BLOCKS, LRU_BW)
    ba, bi = blocks(b_a), blocks(b_i)
    b_cat = jnp.concatenate([ba[0], bi[0], ba[1], bi[1]], axis=-1)
    return w_cat, b_cat


def _stream_layer(h, mod, lw, *, seq, rows_per_mod, ctx):
    tm = 1024
    proj = _norm_matmul(h, mod, lw["g_mix"], [lw["w_in"]], shift_idx=0, scale_idx=1,
                        rows_per_mod=rows_per_mod, out_dtype=F32, tm=tm, tn=512, name="in_proj")
    if ctx is None:
        att, k_new = _context_attention(proj, lw["q_gain"], lw["k_gain"], seq=seq)
        h0 = jnp.zeros((h.shape[0] // seq, 2, D_LRU), F32)
    else:
        k_ctx, v_ctx, h0, bias = ctx
        qn, kn, vb = _qk_norm(proj, lw["q_gain"], lw["k_gain"])
        att = _neighbourhood_attention(qn, kn, vb, k_ctx, v_ctx, bias, seq=seq)
        k_new = None
    lru, lru_final = _rglru(proj, h0, lw["lru_conv_w"], lw["lru_conv_b"], lw["w_cat"], lw["b_cat"],
                            lw["lru_lambda"], seq=seq)
    conv = _short_conv(proj, lw["sc_conv_w"], seq=seq)
    h = _matmul_residual([att, lru, conv], lw["w_out"], h, mod, gate_idx=2,
                         rows_per_mod=rows_per_mod, tm=tm, tn=512, name="out_proj")
    hid = _norm_matmul(h, mod, lw["g_ffn"], [lw["w_ffn_gate"], lw["w_ffn_up"]], shift_idx=3,
                       scale_idx=4, rows_per_mod=rows_per_mod, out_dtype=BF16, tm=tm, tn=512,
                       name="ffn_up")
    h = _matmul_residual([hid], lw["w_ffn_down"], h, mod, gate_idx=5,
                         rows_per_mod=rows_per_mod, tm=tm, tn=256, name="ffn_down")
    return h, proj, k_new, lru_final


def kernel(x_prompt, x_sample, cache_k, cache_v, state_lru, c, c_ctx, w_mod, b_mod, g_mix, g_ffn, w_in, q_gain, k_gain, na_bias, lru_conv_w, lru_conv_b, lru_w_a, lru_b_a, lru_w_i, lru_b_i, lru_lambda, sc_conv_w, w_out, w_ffn_gate, w_ffn_up, w_ffn_down):
    batch, seq, d = x_prompt.shape
    dec_batch, dec_seq, _ = x_sample.shape
    depth = w_mod.shape[0]
    past = cache_k.shape[2]

    cond = jnp.zeros((MOD_ROWS, d), F32).at[0].set(c_ctx).at[1:1 + dec_batch].set(c)
    mod_all = _modulation(cond, w_mod, b_mod).reshape(depth, MOD_ROWS, 6, d)
    bias_all = _na_bias(na_bias)

    hp = x_prompt.reshape(batch * seq, d)
    hs = x_sample.reshape(dec_batch * dec_seq, d)
    new_k, new_v, new_s = [], [], []
    for l in range(depth):
        w_cat, b_cat = _lru_gate_weights(lru_w_a[l], lru_b_a[l], lru_w_i[l], lru_b_i[l])
        lw = {
            "g_mix": g_mix[l], "g_ffn": g_ffn[l], "w_in": w_in[l], "q_gain": q_gain[l],
            "k_gain": k_gain[l], "lru_conv_w": lru_conv_w[l], "lru_conv_b": lru_conv_b[l],
            "w_cat": w_cat, "b_cat": b_cat, "lru_lambda": lru_lambda[l],
            "sc_conv_w": sc_conv_w[l], "w_out": w_out[l], "w_ffn_gate": w_ffn_gate[l],
            "w_ffn_up": w_ffn_up[l], "w_ffn_down": w_ffn_down[l],
        }
        hp, proj_p, k_l, s_l = _stream_layer(hp, mod_all[l, 0:1], lw, seq=seq,
                                             rows_per_mod=batch * seq, ctx=None)
        new_k.append(k_l.reshape(batch, seq, N_HEADS, HEAD_DIM))
        new_v.append(proj_p[:, 2 * D_ATT:3 * D_ATT].reshape(batch, seq, N_HEADS, HEAD_DIM))
        new_s.append(s_l)
        ctx = (cache_k[:, l].reshape(dec_batch, past, D_ATT),
               cache_v[:, l].reshape(dec_batch, past, D_ATT), state_lru[:, l], bias_all[l])
        hs, _, _, _ = _stream_layer(hs, mod_all[l, 1:1 + dec_batch], lw, seq=dec_seq,
                                    rows_per_mod=dec_seq, ctx=ctx)
    return (hp.reshape(batch, seq, d), hs.reshape(dec_batch, dec_seq, d),
            jnp.stack(new_k, axis=1), jnp.stack(new_v, axis=1), jnp.stack(new_s, axis=1))
```

```python
import functools

import jax
import jax.numpy as jnp
from jax import lax
from jax.experimental import pallas as pl
from jax.experimental.pallas import tpu as pltpu

D_MODEL = 2048
N_HEADS = 8
HEAD_DIM = 128
D_ATT = N_HEADS * HEAD_DIM
D_LRU = 512
LRU_BLOCKS = 4
LRU_BW = 128
LRU_CONV = 4
LRU_C = 8.0
D_SC = 512
SC_CONV = 3
GRID_W = 64
NA_ROWS = 8
NA_COLS = 16
EPS = 1e-6
ATT_SCALE = HEAD_DIM ** -0.5

NA_QROWS = 4
NA_KROWS = 12
NA_QBLK = NA_QROWS * GRID_W
NA_KBLK = NA_KROWS * GRID_W
MASK_VALUE = -1e30

LANES = 128
SUBLANES = 8
MOD_ROWS = SUBLANES
VMEM_LIMIT_BYTES = 56 * 1024 * 1024

SEG_LEN = 256
SEG_PER_BLOCK = SUBLANES
MIX_ROWS = SEG_LEN * SEG_PER_BLOCK
PAD = SUBLANES
MM_ROWS = 1024

F32 = jnp.float32
BF16 = jnp.bfloat16


def _params(*semantics):
    return pltpu.CompilerParams(dimension_semantics=semantics,
                                vmem_limit_bytes=VMEM_LIMIT_BYTES)


def _resident(block_shape, index_map):
    return pl.BlockSpec(block_shape, index_map, pipeline_mode=pl.Buffered(1))


def _dot(a, b):
    return jnp.dot(a, b, preferred_element_type=F32)


def _dot_nt(a, b):
    return lax.dot_general(a, b, (((1,), (1,)), ((), ())), preferred_element_type=F32)


def _sigmoid(x):
    return 0.5 * (1.0 + jnp.tanh(0.5 * x))


def _head_rms(x, gain):
    return x * lax.rsqrt(jnp.mean(x * x, axis=-1, keepdims=True) + EPS) * gain


def _mod_kernel(c_ref, w_ref, b_ref, o_ref):
    c = c_ref[...]
    s = (c * _sigmoid(c)).astype(BF16)
    o_ref[0] = _dot(s, w_ref[0].astype(BF16)) + b_ref[0]


def _modulation(cond, w_mod, b_mod, *, tn=1024):
    depth, d, n = w_mod.shape
    return pl.pallas_call(
        _mod_kernel,
        grid=(depth, n // tn),
        in_specs=[
            pl.BlockSpec((MOD_ROWS, d), lambda l, j: (0, 0)),
            pl.BlockSpec((1, d, tn), lambda l, j: (l, 0, j)),
            pl.BlockSpec((1, 1, tn), lambda l, j: (l, 0, j)),
        ],
        out_specs=pl.BlockSpec((1, MOD_ROWS, tn), lambda l, j: (l, 0, j)),
        out_shape=jax.ShapeDtypeStruct((depth, MOD_ROWS, n), F32),
        compiler_params=_params("arbitrary", "arbitrary"),
        name="modulation",
    )(cond, w_mod, b_mod.reshape(depth, 1, n))


def _norm_mm_kernel(x_ref, mod_ref, g_ref, *rest, n_w, shift_idx, scale_idx, n_chunks):
    w_refs, o_ref, u_ref, wb_refs = rest[:n_w], rest[n_w], rest[n_w + 1], rest[n_w + 2:]
    s = pl.program_id(0)
    cm = x_ref.shape[0]

    @pl.when(s < n_chunks)
    def _():
        g = g_ref[0]
        sc = 1.0 + mod_ref[0, scale_idx:scale_idx + 1, :]
        sh = mod_ref[0, shift_idx:shift_idx + 1, :]

        def piece(r, carry):
            x = x_ref[pl.ds(pl.multiple_of(r * LANES, LANES), LANES), :]
            y = x * lax.rsqrt(jnp.mean(x * x, axis=-1, keepdims=True) + EPS) * g
            u_ref[pl.ds(pl.multiple_of(s * cm + r * LANES, LANES), LANES), :] = (y * sc + sh).astype(BF16)
            return carry

        lax.fori_loop(0, cm // LANES, piece, 0)

    @pl.when(s >= n_chunks)
    def _():
        for w_ref, wb_ref in zip(w_refs, wb_refs):
            wb_ref[...] = w_ref[0].astype(BF16)

        def rows_step(r, carry):
            rows = pl.ds(pl.multiple_of(r * MM_ROWS, MM_ROWS), MM_ROWS)
            u = u_ref[rows, :]
            a = _dot(u, wb_refs[0][...])
            if n_w == 2:
                a = a * _sigmoid(a) * _dot(u, wb_refs[1][...])
            o_ref[rows, :] = a.astype(o_ref.dtype)
            return carry

        lax.fori_loop(0, u_ref.shape[0] // MM_ROWS, rows_step, 0)


def _norm_matmul(x, mod, gain, weights, layer, *, shift_idx, scale_idx, rows_per_mod, tn, name,
                 cm=512):
    m, d = x.shape
    n = weights[0].shape[2]
    n_chunks = m // cm
    chunks_per_mod = rows_per_mod // cm
    chunk = lambda s: jnp.minimum(s, n_chunks - 1)
    tile = lambda s: jnp.maximum(s - n_chunks, 0)
    kernel = functools.partial(_norm_mm_kernel, n_w=len(weights), shift_idx=shift_idx,
                               scale_idx=scale_idx, n_chunks=n_chunks)
    return pl.pallas_call(
        kernel,
        grid=(n_chunks + n // tn,),
        in_specs=[
            pl.BlockSpec((cm, d), lambda s: (chunk(s), 0)),
            pl.BlockSpec((1, 6, d), lambda s: (chunk(s) // chunks_per_mod, 0, 0)),
            pl.BlockSpec((1, 1, d), lambda s: (layer, 0, 0)),
        ] + [pl.BlockSpec((1, d, tn), lambda s: (layer, 0, tile(s))) for _ in weights],
        out_specs=pl.BlockSpec((m, tn), lambda s: (0, tile(s))),
        out_shape=jax.ShapeDtypeStruct((m, n), BF16),
        scratch_shapes=[pltpu.VMEM((m, d), BF16)] + [pltpu.VMEM((d, tn), BF16) for _ in weights],
        compiler_params=_params("arbitrary"),
        name=name,
    )(x, mod, gain.reshape(gain.shape[0], 1, d), *weights)


def _mm_res_kernel(*refs, splits, gate_idx, rows_per_mod):
    n = len(splits)
    a_refs = refs[:n]
    w_ref, h_ref, mod_ref, o_ref, wb_ref = refs[n:]
    tm = h_ref.shape[0]
    row0 = pl.program_id(0) * tm
    wb_ref[...] = w_ref[0].astype(BF16)

    def rows_step(r, carry):
        rows = pl.ds(pl.multiple_of(r * MM_ROWS, MM_ROWS), MM_ROWS)
        acc = None
        k0 = 0
        for a_ref, kk in zip(a_refs, splits):
            part = _dot(a_ref[rows, :], wb_ref[k0:k0 + kk, :])
            acc = part if acc is None else acc + part
            k0 += kk
        gate = mod_ref[(row0 + r * MM_ROWS) // rows_per_mod, gate_idx:gate_idx + 1, :]
        o_ref[rows, :] = h_ref[rows, :] + gate * acc
        return carry

    lax.fori_loop(0, tm // MM_ROWS, rows_step, 0)


def _matmul_residual(a_parts, w, layer, h, mod, *, gate_idx, rows_per_mod, tm, tn, name):
    m, n = h.shape
    k = w.shape[1]
    splits = tuple(a.shape[1] for a in a_parts)
    kernel = functools.partial(_mm_res_kernel, splits=splits, gate_idx=gate_idx,
                               rows_per_mod=rows_per_mod)
    return pl.pallas_call(
        kernel,
        grid=(m // tm, n // tn),
        in_specs=[_resident((tm, kk), lambda i, j: (i, 0)) for kk in splits] + [
            pl.BlockSpec((1, k, tn), lambda i, j: (layer, 0, j)),
            pl.BlockSpec((tm, tn), lambda i, j: (i, j)),
            pl.BlockSpec((mod.shape[0], 6, tn), lambda i, j: (0, 0, j)),
        ],
        out_specs=pl.BlockSpec((tm, tn), lambda i, j: (i, j)),
        out_shape=jax.ShapeDtypeStruct((m, n), F32),
        scratch_shapes=[pltpu.VMEM((k, tn), BF16)],
        compiler_params=_params("arbitrary", "arbitrary"),
        name=name,
    )(*a_parts, w, h, mod)


def _ctx_attn_kernel(q_ref, k_ref, v_ref, qg_ref, kg_ref, kacc_ref, vacc_ref,
                     att_ref, kout_ref, vout_ref):
    del kacc_ref, vacc_ref
    qg = qg_ref[0]
    kg = kg_ref[0]
    for h in range(N_HEADS):
        sl = slice(h * HEAD_DIM, (h + 1) * HEAD_DIM)
        qn = _head_rms(q_ref[:, sl].astype(F32), qg)
        kn = _head_rms(k_ref[:, sl].astype(F32), kg)
        v = v_ref[:, sl]
        kout_ref[0, 0, :, sl] = kn
        vout_ref[0, 0, :, sl] = v.astype(F32)
        s = _dot_nt(qn.astype(BF16), kn.astype(BF16)) * ATT_SCALE
        e = jnp.exp(s - jnp.max(s, axis=-1, keepdims=True))
        den = jnp.sum(e, axis=-1, keepdims=True)
        att_ref[:, sl] = (_dot(e.astype(BF16), v) / den).astype(BF16)


def _context_attention(proj, q_gain, k_gain, layer, k_acc, v_acc, *, seq):
    m = proj.shape[0]
    blk = lambda c: pl.BlockSpec((seq, D_ATT), lambda b, c=c: (b, c))
    gain = pl.BlockSpec((1, 1, HEAD_DIM), lambda b: (layer, 0, 0))
    cache = pl.BlockSpec((1, 1, seq, D_ATT), lambda b: (b, layer, 0, 0))
    hbm = pl.BlockSpec(memory_space=pl.ANY)
    gains = lambda g: g.reshape(g.shape[0], 1, HEAD_DIM)
    return pl.pallas_call(
        _ctx_attn_kernel,
        grid=(m // seq,),
        in_specs=[blk(0), blk(1), blk(2), gain, gain, hbm, hbm],
        out_specs=[pl.BlockSpec((seq, D_ATT), lambda b: (b, 0)), cache, cache],
        out_shape=[jax.ShapeDtypeStruct((m, D_ATT), BF16),
                   jax.ShapeDtypeStruct(k_acc.shape, F32),
                   jax.ShapeDtypeStruct(v_acc.shape, F32)],
        input_output_aliases={5: 1, 6: 2},
        compiler_params=_params("arbitrary"),
        name="context_attention",
    )(proj, proj, proj, gains(q_gain), gains(k_gain), k_acc, v_acc)


def _qk_norm_kernel(q_ref, k_ref, qg_ref, kg_ref, qo_ref, ko_ref):
    qg = qg_ref[0]
    kg = kg_ref[0]
    for h in range(N_HEADS):
        sl = slice(h * HEAD_DIM, (h + 1) * HEAD_DIM)
        qo_ref[:, sl] = _head_rms(q_ref[:, sl].astype(F32), qg).astype(BF16)
        ko_ref[:, sl] = _head_rms(k_ref[:, sl].astype(F32), kg).astype(BF16)


def _qk_norm(proj, q_gain, k_gain, layer, *, tm=512):
    m = proj.shape[0]
    blk = lambda c: pl.BlockSpec((tm, D_ATT), lambda i, c=c: (i, c))
    gain = pl.BlockSpec((1, 1, HEAD_DIM), lambda i: (layer, 0, 0))
    out = pl.BlockSpec((tm, D_ATT), lambda i: (i, 0))
    gains = lambda g: g.reshape(g.shape[0], 1, HEAD_DIM)
    return pl.pallas_call(
        _qk_norm_kernel,
        grid=(m // tm,),
        in_specs=[blk(0), blk(1), gain, gain],
        out_specs=[out, out],
        out_shape=[jax.ShapeDtypeStruct((m, D_ATT), BF16)] * 2,
        compiler_params=_params("arbitrary"),
        name="qk_norm",
    )(proj, proj, gains(q_gain), gains(k_gain))


def _na_block_geometry():
    rows = 32
    return ((0, 0), (NA_QROWS, 0), (rows - NA_QROWS, rows - NA_KROWS)), rows


def _na_bias_kernel(rel_ref, o_ref):
    qc = lax.broadcasted_iota(jnp.int32, (GRID_W, LANES), 0)
    kc = lax.broadcasted_iota(jnp.int32, (GRID_W, LANES), 1)
    cs = jnp.clip(qc - NA_COLS // 2, 0, GRID_W - NA_COLS)
    col_ok = (kc >= cs) & (kc < cs + NA_COLS)
    tiles = []
    for dr in range(2 * NA_ROWS - 1):
        x = jnp.broadcast_to(rel_ref[0, 0, dr:dr + 1, :], (GRID_W, LANES))
        for bit in range(6):
            x = jnp.where(((qc >> bit) & 1) == 1, pltpu.roll(x, 1 << bit, axis=1), x)
        tiles.append(jnp.where(col_ok, x, MASK_VALUE)[:, :GRID_W])
    masked = jnp.full((GRID_W, GRID_W), MASK_VALUE, F32)
    kinds, rows = _na_block_geometry()
    for kind, (r0, kb) in enumerate(kinds):
        for qr in range(NA_QROWS):
            r = r0 + qr
            rs = min(max(r - NA_ROWS // 2, 0), rows - NA_ROWS)
            for kr in range(NA_KROWS):
                krow = kb + kr
                tile = tiles[krow - r + NA_ROWS - 1] if rs <= krow < rs + NA_ROWS else masked
                o_ref[0, kind, 0, qr * GRID_W:(qr + 1) * GRID_W,
                      kr * GRID_W:(kr + 1) * GRID_W] = tile


def _na_bias(na_bias):
    depth = na_bias.shape[0]
    n_dr = 2 * NA_ROWS - 1
    rel = jnp.pad(na_bias, ((0, 0), (0, 0), (0, 0), (0, LANES - (2 * NA_COLS - 1))))
    rel = jnp.roll(rel, -(NA_COLS - 1), axis=-1)
    return pl.pallas_call(
        _na_bias_kernel,
        grid=(depth, N_HEADS),
        in_specs=[pl.BlockSpec((1, 1, n_dr, LANES), lambda l, h: (l, h, 0, 0))],
        out_specs=pl.BlockSpec((1, 3, 1, NA_QBLK, NA_KBLK), lambda l, h: (l, 0, h, 0, 0)),
        out_shape=jax.ShapeDtypeStruct((depth, 3, N_HEADS, NA_QBLK, NA_KBLK), F32),
        compiler_params=_params("arbitrary", "arbitrary"),
        name="na_bias",
    )(rel)


def _na_attn_kernel(q_ref, k_ref, v_ref, kc_ref, vc_ref, bias_ref, o_ref, *, n_blocks):
    blk = pl.program_id(1)
    start = pl.multiple_of(jnp.clip(blk - 1, 0, n_blocks - 3) * NA_QBLK, NA_QBLK)
    for h in range(N_HEADS):
        sl = slice(h * HEAD_DIM, (h + 1) * HEAD_DIM)
        q = q_ref[:, sl]
        kw = k_ref[pl.ds(start, NA_KBLK), sl]
        vw = v_ref[pl.ds(start, NA_KBLK), sl]
        s_loc = _dot_nt(q, kw) * ATT_SCALE + bias_ref[0, 0, h]
        s_ctx = _dot_nt(q, kc_ref[0, 0, :, sl].astype(BF16)) * ATT_SCALE
        mx = jnp.maximum(jnp.max(s_loc, axis=-1, keepdims=True),
                         jnp.max(s_ctx, axis=-1, keepdims=True))
        e_loc = jnp.exp(s_loc - mx)
        e_ctx = jnp.exp(s_ctx - mx)
        den = jnp.sum(e_loc, axis=-1, keepdims=True) + jnp.sum(e_ctx, axis=-1, keepdims=True)
        o = _dot(e_loc.astype(BF16), vw) + _dot(e_ctx.astype(BF16), vc_ref[0, 0, :, sl].astype(BF16))
        o_ref[:, sl] = (o / den).astype(BF16)


def _neighbourhood_attention(qn, kn, proj, k_ctx, v_ctx, bias, layer, *, seq):
    m = qn.shape[0]
    batch = m // seq
    n_blocks = seq // NA_QBLK
    past = k_ctx.shape[2]
    kind = lambda i: jnp.where(i == 0, 0, jnp.where(i == n_blocks - 1, 2, 1))
    ctx = pl.BlockSpec((1, 1, past, D_ATT), lambda b, i: (b, layer, 0, 0))
    return pl.pallas_call(
        functools.partial(_na_attn_kernel, n_blocks=n_blocks),
        grid=(batch, n_blocks),
        in_specs=[
            pl.BlockSpec((NA_QBLK, D_ATT), lambda b, i: (b * n_blocks + i, 0)),
            pl.BlockSpec((seq, D_ATT), lambda b, i: (b, 0)),
            pl.BlockSpec((seq, D_ATT), lambda b, i: (b, 2)),
            ctx, ctx,
            pl.BlockSpec((1, 1, N_HEADS, NA_QBLK, NA_KBLK), lambda b, i: (layer, kind(i), 0, 0, 0)),
        ],
        out_specs=pl.BlockSpec((NA_QBLK, D_ATT), lambda b, i: (b * n_blocks + i, 0)),
        out_shape=jax.ShapeDtypeStruct((m, D_ATT), BF16),
        compiler_params=_params("arbitrary", "arbitrary"),
        name="neighbourhood_attention",
    )(qn, kn, proj, k_ctx, v_ctx, bias)


def _stage_padded(pad_ref, x):
    zeros = jnp.zeros((PAD, pad_ref.shape[1]), F32)
    rows = x.shape[0]
    pad_ref[0:PAD, :] = zeros
    pad_ref[PAD + rows:2 * PAD + rows, :] = zeros
    pad_ref[PAD:PAD + rows, :] = x


def _conv_taps(pad_ref, seg, w, left, isolated):
    base = seg * SEG_LEN
    t = lax.broadcasted_iota(jnp.int32, (SEG_LEN, 1), 0)
    acc = None
    for k in range(w.shape[0]):
        off = k - left
        xs = pad_ref[PAD + base + off:PAD + base + off + SEG_LEN, :]
        if isolated and off != 0:
            xs = jnp.where((t + off >= 0) & (t + off < SEG_LEN), xs, 0.0)
        term = xs * w[k:k + 1, :]
        acc = term if acc is None else acc + term
    return acc


def _gelu_tanh(x):
    return x * (0.5 * (1.0 + jnp.tanh(0.7978845608028654 * (x + 0.044715 * (x * x * x)))))


def _lru_kernel(lx_ref, ly_ref, h0_ref, cw_ref, cb_ref, w_ref, b_ref, lam_ref, *rest, chained):
    if chained:
        y_ref, xpad, a_tm, u_tm, p_tm = rest
    else:
        y_ref, fin_ref, xpad, a_tm, u_tm = rest
    last = (SEG_LEN - 1) * SEG_PER_BLOCK
    _stage_padded(xpad, lx_ref[...].astype(F32))
    cw = cw_ref[0]
    neg_lam = -lam_ref[0]
    softplus = jnp.maximum(neg_lam, 0.0) + jnp.log1p(jnp.exp(-jnp.abs(neg_lam)))
    for c in range(SEG_PER_BLOCK):
        xc = cb_ref[0] + _conv_taps(xpad, c, cw, LRU_CONV // 2, not chained)
        seg_rows = pl.ds(c, SEG_LEN, stride=SEG_PER_BLOCK)
        for n in range(LRU_BLOCKS):
            cols = slice(n * LRU_BW, (n + 1) * LRU_BW)
            xb = xc[:, cols]
            g = _dot(xb.astype(BF16), w_ref[0, n].astype(BF16)) + b_ref[0, n:n + 1, :]
            for d in range(2):
                r = _sigmoid(g[:, (2 * d) * LRU_BW:(2 * d + 1) * LRU_BW])
                i = _sigmoid(g[:, (2 * d + 1) * LRU_BW:(2 * d + 2) * LRU_BW])
                log_a = -LRU_C * r * softplus[d:d + 1, cols]
                th = jnp.tanh(log_a)
                a_tm[d, n, seg_rows, :] = jnp.exp(log_a)
                u_tm[d, n, seg_rows, :] = jnp.sqrt(-2.0 * th / (1.0 - th)) * (i * xb)

    def scan_dir(d, rows, h, p):
        h_new, p_new = [], []
        for n in range(LRU_BLOCKS):
            a = a_tm[d, n, rows, :]
            hn = a * h[n] + u_tm[d, n, rows, :]
            u_tm[d, n, rows, :] = hn
            h_new.append(hn)
            if chained:
                pn = a * p[n]
                p_tm[d, n, rows, :] = pn
                p_new.append(pn)
        return tuple(h_new), tuple(p_new)

    def step(t, carry):
        hf, pf, hb, pb = carry
        hf, pf = scan_dir(0, pl.ds(pl.multiple_of(t * SEG_PER_BLOCK, SEG_PER_BLOCK), SEG_PER_BLOCK), hf, pf)
        tb = SEG_LEN - 1 - t
        hb, pb = scan_dir(1, pl.ds(pl.multiple_of(tb * SEG_PER_BLOCK, SEG_PER_BLOCK), SEG_PER_BLOCK), hb, pb)
        return hf, pf, hb, pb

    lane_blocks = [slice(n * LRU_BW, (n + 1) * LRU_BW) for n in range(LRU_BLOCKS)]
    if chained:
        zero = tuple(jnp.zeros((SEG_PER_BLOCK, LRU_BW), F32) for _ in lane_blocks)
        one = tuple(jnp.ones((SEG_PER_BLOCK, LRU_BW), F32) for _ in lane_blocks)
        init = (zero, one, zero, one)
    else:
        init = (tuple(h0_ref[0, :, cols] for cols in lane_blocks), (),
                tuple(h0_ref[1, :, cols] for cols in lane_blocks), ())
    lax.fori_loop(0, SEG_LEN, step, init, unroll=4)

    carry_f = carry_b = None
    if chained:
        carry_f, carry_b = [], []
        for n, cols in enumerate(lane_blocks):
            end_h, end_p = u_tm[0, n, last:last + SEG_PER_BLOCK, :], p_tm[0, n, last:last + SEG_PER_BLOCK, :]
            cf = [h0_ref[0, 0:1, cols]]
            for s in range(SEG_PER_BLOCK - 1):
                cf.append(end_h[s:s + 1] + end_p[s:s + 1] * cf[s])
            carry_f.append(cf)
            beg_h, beg_p = u_tm[1, n, 0:SEG_PER_BLOCK, :], p_tm[1, n, 0:SEG_PER_BLOCK, :]
            cb = [None] * SEG_PER_BLOCK
            cb[SEG_PER_BLOCK - 1] = h0_ref[0, 1:2, cols]
            for s in range(SEG_PER_BLOCK - 1, 0, -1):
                cb[s - 1] = beg_h[s:s + 1] + beg_p[s:s + 1] * cb[s]
            carry_b.append(cb)
    else:
        for n, cols in enumerate(lane_blocks):
            fin_ref[0, :, cols] = u_tm[0, n, last:last + SEG_PER_BLOCK, :]
            fin_ref[1, :, cols] = u_tm[1, n, 0:SEG_PER_BLOCK, :]

    for c in range(SEG_PER_BLOCK):
        seg_rows = pl.ds(c, SEG_LEN, stride=SEG_PER_BLOCK)
        tok = slice(c * SEG_LEN, (c + 1) * SEG_LEN)
        for n, cols in enumerate(lane_blocks):
            h = u_tm[0, n, seg_rows, :] + u_tm[1, n, seg_rows, :]
            if chained:
                h = h + p_tm[0, n, seg_rows, :] * carry_f[n][c] + p_tm[1, n, seg_rows, :] * carry_b[n][c]
            y_ref[tok, cols] = (h * _gelu_tanh(ly_ref[tok, cols].astype(F32))).astype(BF16)


def _rglru(proj, h0, conv_w, conv_b, w_cat, b_cat, lam, layer, *, chained):
    m = proj.shape[0]
    blocks = m // MIX_ROWS
    lx_col = 3 * D_ATT // D_LRU
    per_layer = lambda *shape: pl.BlockSpec((1,) + shape, lambda b: (layer,) + (0,) * len(shape))
    tm_buf = pltpu.VMEM((2, LRU_BLOCKS, MIX_ROWS, LRU_BW), F32)
    if chained:
        h0_spec = pl.BlockSpec((1, 2, D_LRU), lambda b: (b, 0, 0))
        out_specs = [pl.BlockSpec((MIX_ROWS, D_LRU), lambda b: (b, 0))]
        out_shape = [jax.ShapeDtypeStruct((m, D_LRU), BF16)]
        scratch = [tm_buf, tm_buf, tm_buf]
    else:
        h0_spec = pl.BlockSpec((2, SEG_PER_BLOCK, D_LRU), lambda b: (0, b, 0))
        out_specs = [pl.BlockSpec((MIX_ROWS, D_LRU), lambda b: (b, 0)),
                     pl.BlockSpec((2, SEG_PER_BLOCK, D_LRU), lambda b: (0, b, 0))]
        out_shape = [jax.ShapeDtypeStruct((m, D_LRU), BF16),
                     jax.ShapeDtypeStruct((2, blocks * SEG_PER_BLOCK, D_LRU), F32)]
        scratch = [tm_buf, tm_buf]
    return pl.pallas_call(
        functools.partial(_lru_kernel, chained=chained),
        grid=(blocks,),
        in_specs=[
            pl.BlockSpec((MIX_ROWS, D_LRU), lambda b: (b, lx_col)),
            pl.BlockSpec((MIX_ROWS, D_LRU), lambda b: (b, lx_col + 1)),
            h0_spec,
            per_layer(LRU_CONV, D_LRU), per_layer(1, D_LRU),
            per_layer(LRU_BLOCKS, LRU_BW, 4 * LRU_BW), per_layer(LRU_BLOCKS, 4 * LRU_BW),
            per_layer(2, D_LRU),
        ],
        out_specs=out_specs,
        out_shape=out_shape,
        scratch_shapes=[pltpu.VMEM((MIX_ROWS + 2 * PAD, D_LRU), F32)] + scratch,
        compiler_params=_params("arbitrary"),
        name="rglru",
    )(proj, proj, h0, conv_w, conv_b.reshape(conv_b.shape[0], 1, D_LRU), w_cat, b_cat, lam)


def _sconv_kernel(b_ref, c_ref, x_ref, w_ref, y_ref, pad, *, isolated):
    _stage_padded(pad, c_ref[...].astype(F32) * x_ref[...].astype(F32))
    w = w_ref[0]
    for c in range(SEG_PER_BLOCK):
        tok = slice(c * SEG_LEN, (c + 1) * SEG_LEN)
        conv = _conv_taps(pad, c, w, SC_CONV // 2, isolated)
        y_ref[tok, :] = (b_ref[tok, :].astype(F32) * conv).astype(BF16)


def _short_conv(proj, w, layer, *, isolated):
    m = proj.shape[0]
    col0 = (3 * D_ATT + 2 * D_LRU) // D_SC
    blk = lambda part: pl.BlockSpec((MIX_ROWS, D_SC), lambda b, part=part: (b, col0 + part))
    return pl.pallas_call(
        functools.partial(_sconv_kernel, isolated=isolated),
        grid=(m // MIX_ROWS,),
        in_specs=[blk(0), blk(1), blk(2), pl.BlockSpec((1, SC_CONV, D_SC), lambda b: (layer, 0, 0))],
        out_specs=pl.BlockSpec((MIX_ROWS, D_SC), lambda b: (b, 0)),
        out_shape=jax.ShapeDtypeStruct((m, D_SC), BF16),
        scratch_shapes=[pltpu.VMEM((MIX_ROWS + 2 * PAD, D_SC), F32)],
        compiler_params=_params("arbitrary"),
        name="short_conv",
    )(proj, proj, proj, w)


def _lru_gate_weights(w_a, b_a, w_i, b_i):
    depth = w_a.shape[0]
    w_cat = jnp.concatenate([w_a[:, 0], w_i[:, 0], w_a[:, 1], w_i[:, 1]], axis=-1)
    blocks = lambda b: b.reshape(depth, 2, LRU_BLOCKS, LRU_BW)
    ba, bi = blocks(b_a), blocks(b_i)
    b_cat = jnp.concatenate([ba[:, 0], bi[:, 0], ba[:, 1], bi[:, 1]], axis=-1)
    return w_cat, b_cat


def _stream_layer(h, mod, p, layer, *, rows_per_mod, ctx, caches):
    m = h.shape[0]
    proj = _norm_matmul(h, mod, p["g_mix"], [p["w_in"]], layer, shift_idx=0, scale_idx=1,
                        rows_per_mod=rows_per_mod, tn=512, name="in_proj")
    if ctx is None:
        k_acc, v_acc = caches
        att, k_acc, v_acc = _context_attention(proj, p["q_gain"], p["k_gain"], layer, k_acc, v_acc,
                                               seq=SEG_LEN)
        caches = (k_acc, v_acc)
        h0 = jnp.zeros((2, m // SEG_LEN, D_LRU), F32)
        lru, lru_final = _rglru(proj, h0, p["lru_conv_w"], p["lru_conv_b"], p["w_cat"], p["b_cat"],
                                p["lru_lambda"], layer, chained=False)
    else:
        k_ctx, v_ctx, h0, bias = ctx
        qn, kn = _qk_norm(proj, p["q_gain"], p["k_gain"], layer)
        att = _neighbourhood_attention(qn, kn, proj, k_ctx, v_ctx, bias, layer, seq=MIX_ROWS)
        (lru,) = _rglru(proj, h0, p["lru_conv_w"], p["lru_conv_b"], p["w_cat"], p["b_cat"],
                        p["lru_lambda"], layer, chained=True)
        lru_final = None
    conv = _short_conv(proj, p["sc_conv_w"], layer, isolated=ctx is None)
    h = _matmul_residual([att, lru, conv], p["w_out"], layer, h, mod, gate_idx=2,
                         rows_per_mod=rows_per_mod, tm=m, tn=256, name="out_proj")
    hid = _norm_matmul(h, mod, p["g_ffn"], [p["w_ffn_gate"], p["w_ffn_up"]], layer, shift_idx=3,
                       scale_idx=4, rows_per_mod=rows_per_mod, tn=256, name="ffn_up")
    h = _matmul_residual([hid], p["w_ffn_down"], layer, h, mod, gate_idx=5,
                         rows_per_mod=rows_per_mod, tm=m // 2, tn=256, name="ffn_down")
    return h, caches, lru_final


def kernel(x_prompt, x_sample, cache_k, cache_v, state_lru, c, c_ctx, w_mod, b_mod, g_mix, g_ffn, w_in, q_gain, k_gain, na_bias, lru_conv_w, lru_conv_b, lru_w_a, lru_b_a, lru_w_i, lru_b_i, lru_lambda, sc_conv_w, w_out, w_ffn_gate, w_ffn_up, w_ffn_down):
    batch, seq, d = x_prompt.shape
    dec_batch, dec_seq, _ = x_sample.shape
    depth = w_mod.shape[0]
    past = cache_k.shape[2]
    assert seq == SEG_LEN and dec_seq == MIX_ROWS and batch % SEG_PER_BLOCK == 0

    cond = jnp.zeros((MOD_ROWS, d), F32).at[0].set(c_ctx).at[1:1 + dec_batch].set(c)
    mod_all = _modulation(cond, w_mod, b_mod).reshape(depth, MOD_ROWS, 6, d)
    bias_all = _na_bias(na_bias)
    w_cat, b_cat = _lru_gate_weights(lru_w_a, lru_b_a, lru_w_i, lru_b_i)
    p = {
        "g_mix": g_mix, "g_ffn": g_ffn, "w_in": w_in, "q_gain": q_gain, "k_gain": k_gain,
        "lru_conv_w": lru_conv_w, "lru_conv_b": lru_conv_b, "w_cat": w_cat, "b_cat": b_cat,
        "lru_lambda": lru_lambda, "sc_conv_w": sc_conv_w, "w_out": w_out,
        "w_ffn_gate": w_ffn_gate, "w_ffn_up": w_ffn_up, "w_ffn_down": w_ffn_down,
    }
    k_ctx = cache_k.reshape(dec_batch, depth, past, D_ATT)
    v_ctx = cache_v.reshape(dec_batch, depth, past, D_ATT)

    hp = x_prompt.reshape(batch * seq, d)
    hs = x_sample.reshape(dec_batch * dec_seq, d)
    caches = (jnp.zeros((batch, depth, seq, D_ATT), F32), jnp.zeros((batch, depth, seq, D_ATT), F32))
    new_s = []
    for l in range(depth):
        hp, caches, s_l = _stream_layer(hp, mod_all[l, 0:1], p, l, rows_per_mod=batch * seq,
                                        ctx=None, caches=caches)
        new_s.append(jnp.swapaxes(s_l, 0, 1))
        ctx = (k_ctx, v_ctx, state_lru[:, l], bias_all)
        hs, _, _ = _stream_layer(hs, mod_all[l, 1:1 + dec_batch], p, l, rows_per_mod=dec_seq,
                                 ctx=ctx, caches=None)
    cache_shape = (batch, depth, seq, N_HEADS, HEAD_DIM)
    return (hp.reshape(batch, seq, d), hs.reshape(dec_batch, dec_seq, d),
            caches[0].reshape(cache_shape), caches[1].reshape(cache_shape), jnp.stack(new_s, axis=1))
```

```python
import functools

import jax
import jax.numpy as jnp
from jax import lax
from jax.experimental import pallas as pl
from jax.experimental.pallas import tpu as pltpu

D_MODEL = 2048
N_HEADS = 8
HEAD_DIM = 128
D_ATT = N_HEADS * HEAD_DIM
D_LRU = 512
LRU_BLOCKS = 4
LRU_BW = 128
LRU_CONV = 4
LRU_C = 8.0
D_SC = 512
SC_CONV = 3
GRID_W = 64
NA_ROWS = 8
NA_COLS = 16
EPS = 1e-6
ATT_SCALE = HEAD_DIM ** -0.5

NA_QROWS = 4
NA_KROWS = 12
NA_QBLK = NA_QROWS * GRID_W
NA_KBLK = NA_KROWS * GRID_W
MASK_VALUE = -1e30

LANES = 128
SUBLANES = 8
MOD_ROWS = SUBLANES
VMEM_LIMIT_BYTES = 56 * 1024 * 1024

SEG_LEN = 256
SEG_PER_BLOCK = SUBLANES
MIX_ROWS = SEG_LEN * SEG_PER_BLOCK
PAD = SUBLANES
MM_ROWS = 1024

F32 = jnp.float32
BF16 = jnp.bfloat16


def _params(*semantics):
    return pltpu.CompilerParams(dimension_semantics=semantics,
                                vmem_limit_bytes=VMEM_LIMIT_BYTES)


def _dot(a, b):
    return jnp.dot(a, b, preferred_element_type=F32)


def _dot_nt(a, b):
    return lax.dot_general(a, b, (((1,), (1,)), ((), ())), preferred_element_type=F32)


def _sigmoid(x):
    return 0.5 * (1.0 + jnp.tanh(0.5 * x))


def _mod_kernel(c_ref, w_ref, b_ref, o_ref):
    c = c_ref[...]
    s = (c * _sigmoid(c)).astype(BF16)
    o_ref[0] = _dot(s, w_ref[0].astype(BF16)) + b_ref[0]


def _modulation(cond, w_mod, b_mod, *, tn=1024):
    depth, d, n = w_mod.shape
    return pl.pallas_call(
        _mod_kernel,
        grid=(depth, n // tn),
        in_specs=[
            pl.BlockSpec((MOD_ROWS, d), lambda l, j: (0, 0)),
            pl.BlockSpec((1, d, tn), lambda l, j: (l, 0, j)),
            pl.BlockSpec((1, 1, tn), lambda l, j: (l, 0, j)),
        ],
        out_specs=pl.BlockSpec((1, MOD_ROWS, tn), lambda l, j: (l, 0, j)),
        out_shape=jax.ShapeDtypeStruct((depth, MOD_ROWS, n), F32),
        compiler_params=_params("arbitrary", "arbitrary"),
        name="modulation",
    )(cond, w_mod, b_mod.reshape(depth, 1, n))


def _norm_mm_kernel(x_ref, mod_ref, g_ref, *rest, n_w, shift_idx, scale_idx, n_chunks):
    w_refs, o_ref, u_ref, wb_refs = rest[:n_w], rest[n_w], rest[n_w + 1], rest[n_w + 2:]
    s = pl.program_id(0)
    cm = x_ref.shape[0]

    def cast_weights():
        for w_ref, wb_ref in zip(w_refs, wb_refs):
            wb_ref[...] = w_ref[0].astype(BF16)

    def matmul_rows(row0, rows):
        sl = pl.ds(pl.multiple_of(row0, rows), rows)
        u = u_ref[sl, :]
        a = _dot(u, wb_refs[0][...])
        if n_w == 2:
            a = a * _sigmoid(a) * _dot(u, wb_refs[1][...])
        o_ref[sl, :] = a.astype(o_ref.dtype)

    def norm_chunk():
        gain = g_ref[0] * (1.0 + mod_ref[0, scale_idx:scale_idx + 1, :])
        shift = mod_ref[0, shift_idx:shift_idx + 1, :]
        for p in range(cm // LANES):
            x = x_ref[p * LANES:(p + 1) * LANES, :]
            y = x * lax.rsqrt(jnp.mean(x * x, axis=-1, keepdims=True) + EPS)
            u_ref[pl.ds(pl.multiple_of(s * cm + p * LANES, LANES), LANES), :] = (y * gain + shift).astype(BF16)

    @pl.when(s == 0)
    def _():
        cast_weights()
        norm_chunk()

    @pl.when((s >= 1) & (s < n_chunks))
    def _():
        matmul_rows((s - 1) * cm, cm)
        norm_chunk()

    @pl.when(s == n_chunks)
    def _():
        matmul_rows((n_chunks - 1) * cm, cm)

    @pl.when(s > n_chunks)
    def _():
        cast_weights()

        def rows_step(r, carry):
            matmul_rows(r * MM_ROWS, MM_ROWS)
            return carry

        lax.fori_loop(0, u_ref.shape[0] // MM_ROWS, rows_step, 0)


def _norm_matmul(x, mod, gain, weights, layer, *, shift_idx, scale_idx, rows_per_mod, tn, name,
                 cm=512):
    m, d = x.shape
    n = weights[0].shape[2]
    n_chunks = m // cm
    chunks_per_mod = rows_per_mod // cm
    chunk = lambda s: jnp.minimum(s, n_chunks - 1)
    tile = lambda s: jnp.maximum(s - n_chunks, 0)
    kernel = functools.partial(_norm_mm_kernel, n_w=len(weights), shift_idx=shift_idx,
                               scale_idx=scale_idx, n_chunks=n_chunks)
    return pl.pallas_call(
        kernel,
        grid=(n_chunks + n // tn,),
        in_specs=[
            pl.BlockSpec((cm, d), lambda s: (chunk(s), 0)),
            pl.BlockSpec((1, 6, d), lambda s: (chunk(s) // chunks_per_mod, 0, 0)),
            pl.BlockSpec((1, 1, d), lambda s: (layer, 0, 0)),
        ] + [pl.BlockSpec((1, d, tn), lambda s: (layer, 0, tile(s))) for _ in weights],
        out_specs=pl.BlockSpec((m, tn), lambda s: (0, tile(s))),
        out_shape=jax.ShapeDtypeStruct((m, n), BF16),
        scratch_shapes=[pltpu.VMEM((m, d), BF16)] + [pltpu.VMEM((d, tn), BF16) for _ in weights],
        compiler_params=_params("arbitrary"),
        name=name,
    )(x, mod, gain.reshape(gain.shape[0], 1, d), *weights)


def _mm_res_kernel(*refs, splits, gate_idx, rows_per_mod, n_chunks):
    n = len(splits)
    a_refs = refs[:n]
    w_ref, h_ref, mod_ref, o_ref, a_res, wb_ref = refs[n:]
    s = pl.program_id(1)
    tm = h_ref.shape[0]
    ca = a_refs[0].shape[0]
    tile_row0 = pl.program_id(0) * tm

    def product(row0, rows):
        sl = pl.ds(pl.multiple_of(row0, rows), rows)
        gate = mod_ref[(tile_row0 + row0) // rows_per_mod, gate_idx:gate_idx + 1, :]
        o_ref[sl, :] = h_ref[sl, :] + gate * _dot(a_res[sl, :], wb_ref[...])

    @pl.when((s == 0) | (s >= n_chunks))
    def _():
        wb_ref[...] = w_ref[0].astype(BF16)

    @pl.when(s < n_chunks)
    def _():
        rows = pl.ds(pl.multiple_of(s * ca, ca), ca)
        k0 = 0
        for a_ref, kk in zip(a_refs, splits):
            a_res[rows, k0:k0 + kk] = a_ref[...]
            k0 += kk
        product(s * ca, ca)

    @pl.when(s >= n_chunks)
    def _():
        def rows_step(r, carry):
            product(r * MM_ROWS, MM_ROWS)
            return carry

        lax.fori_loop(0, tm // MM_ROWS, rows_step, 0)


def _matmul_residual(a_parts, w, layer, h, mod, *, gate_idx, rows_per_mod, tm, tn, ca, name):
    m, n = h.shape
    k = w.shape[1]
    splits = tuple(a.shape[1] for a in a_parts)
    n_chunks = tm // ca
    chunk = lambda i, s: i * n_chunks + jnp.minimum(s, n_chunks - 1)
    tile = lambda s: jnp.maximum(s - n_chunks + 1, 0)
    kernel = functools.partial(_mm_res_kernel, splits=splits, gate_idx=gate_idx,
                               rows_per_mod=rows_per_mod, n_chunks=n_chunks)
    return pl.pallas_call(
        kernel,
        grid=(m // tm, n_chunks + n // tn - 1),
        in_specs=[pl.BlockSpec((ca, kk), lambda i, s: (chunk(i, s), 0)) for kk in splits] + [
            pl.BlockSpec((1, k, tn), lambda i, s: (layer, 0, tile(s))),
            pl.BlockSpec((tm, tn), lambda i, s: (i, tile(s))),
            pl.BlockSpec((mod.shape[0], 6, tn), lambda i, s: (0, 0, tile(s))),
        ],
        out_specs=pl.BlockSpec((tm, tn), lambda i, s: (i, tile(s))),
        out_shape=jax.ShapeDtypeStruct((m, n), F32),
        scratch_shapes=[pltpu.VMEM((tm, k), BF16), pltpu.VMEM((k, tn), BF16)],
        compiler_params=_params("arbitrary", "arbitrary"),
        name=name,
    )(*a_parts, w, h, mod)


def _head_rms(x, gain):
    return x * lax.rsqrt(jnp.mean(x * x, axis=-1, keepdims=True) + EPS) * gain


def _ctx_attn_kernel(q_ref, k_ref, v_ref, qg_ref, kg_ref, kacc_ref, vacc_ref,
                     att_ref, kout_ref, vout_ref):
    del kacc_ref, vacc_ref
    qg = qg_ref[0]
    kg = kg_ref[0]
    for h in range(N_HEADS):
        sl = slice(h * HEAD_DIM, (h + 1) * HEAD_DIM)
        qn = _head_rms(q_ref[:, sl].astype(F32), qg)
        kn = _head_rms(k_ref[:, sl].astype(F32), kg)
        v = v_ref[:, sl]
        kout_ref[0, 0, :, sl] = kn
        vout_ref[0, 0, :, sl] = v.astype(F32)
        s = _dot_nt(qn.astype(BF16), kn.astype(BF16)) * ATT_SCALE
        e = jnp.exp(s - jnp.max(s, axis=-1, keepdims=True))
        den = jnp.sum(e, axis=-1, keepdims=True)
        att_ref[:, sl] = (_dot(e.astype(BF16), v) / den).astype(BF16)


def _context_attention(proj, q_gain, k_gain, layer, k_acc, v_acc, *, seq):
    m = proj.shape[0]
    blk = lambda c: pl.BlockSpec((seq, D_ATT), lambda b, c=c: (b, c))
    gain = pl.BlockSpec((1, 1, HEAD_DIM), lambda b: (layer, 0, 0))
    cache = pl.BlockSpec((1, 1, seq, D_ATT), lambda b: (b, layer, 0, 0))
    hbm = pl.BlockSpec(memory_space=pl.ANY)
    gains = lambda g: g.reshape(g.shape[0], 1, HEAD_DIM)
    return pl.pallas_call(
        _ctx_attn_kernel,
        grid=(m // seq,),
        in_specs=[blk(0), blk(1), blk(2), gain, gain, hbm, hbm],
        out_specs=[pl.BlockSpec((seq, D_ATT), lambda b: (b, 0)), cache, cache],
        out_shape=[jax.ShapeDtypeStruct((m, D_ATT), BF16),
                   jax.ShapeDtypeStruct(k_acc.shape, F32),
                   jax.ShapeDtypeStruct(v_acc.shape, F32)],
        input_output_aliases={5: 1, 6: 2},
        compiler_params=_params("arbitrary"),
        name="context_attention",
    )(proj, proj, proj, gains(q_gain), gains(k_gain), k_acc, v_acc)


def _qk_norm_kernel(q_ref, k_ref, qg_ref, kg_ref, qo_ref, ko_ref):
    qg = qg_ref[0]
    kg = kg_ref[0]
    for h in range(N_HEADS):
        sl = slice(h * HEAD_DIM, (h + 1) * HEAD_DIM)
        qo_ref[:, sl] = _head_rms(q_ref[:, sl].astype(F32), qg).astype(BF16)
        ko_ref[:, sl] = _head_rms(k_ref[:, sl].astype(F32), kg).astype(BF16)


def _qk_norm(proj, q_gain, k_gain, layer, *, tm=512):
    m = proj.shape[0]
    blk = lambda c: pl.BlockSpec((tm, D_ATT), lambda i, c=c: (i, c))
    gain = pl.BlockSpec((1, 1, HEAD_DIM), lambda i: (layer, 0, 0))
    out = pl.BlockSpec((tm, D_ATT), lambda i: (i, 0))
    gains = lambda g: g.reshape(g.shape[0], 1, HEAD_DIM)
    return pl.pallas_call(
        _qk_norm_kernel,
        grid=(m // tm,),
        in_specs=[blk(0), blk(1), gain, gain],
        out_specs=[out, out],
        out_shape=[jax.ShapeDtypeStruct((m, D_ATT), BF16)] * 2,
        compiler_params=_params("arbitrary"),
        name="qk_norm",
    )(proj, proj, gains(q_gain), gains(k_gain))


def _na_block_geometry():
    rows = 32
    return ((0, 0), (NA_QROWS, 0), (rows - NA_QROWS, rows - NA_KROWS)), rows


def _na_bias_kernel(rel_ref, o_ref):
    qc = lax.broadcasted_iota(jnp.int32, (GRID_W, LANES), 0)
    kc = lax.broadcasted_iota(jnp.int32, (GRID_W, LANES), 1)
    cs = jnp.clip(qc - NA_COLS // 2, 0, GRID_W - NA_COLS)
    col_ok = (kc >= cs) & (kc < cs + NA_COLS)
    tiles = []
    for dr in range(2 * NA_ROWS - 1):
        x = jnp.broadcast_to(rel_ref[0, 0, dr:dr + 1, :], (GRID_W, LANES))
        for bit in range(6):
            x = jnp.where(((qc >> bit) & 1) == 1, pltpu.roll(x, 1 << bit, axis=1), x)
        tiles.append(jnp.where(col_ok, x, MASK_VALUE)[:, :GRID_W])
    masked = jnp.full((GRID_W, GRID_W), MASK_VALUE, F32)
    kinds, rows = _na_block_geometry()
    for kind, (r0, kb) in enumerate(kinds):
        for qr in range(NA_QROWS):
            r = r0 + qr
            rs = min(max(r - NA_ROWS // 2, 0), rows - NA_ROWS)
            for kr in range(NA_KROWS):
                krow = kb + kr
                tile = tiles[krow - r + NA_ROWS - 1] if rs <= krow < rs + NA_ROWS else masked
                o_ref[0, kind, 0, qr * GRID_W:(qr + 1) * GRID_W,
                      kr * GRID_W:(kr + 1) * GRID_W] = tile


def _na_bias(na_bias):
    depth = na_bias.shape[0]
    n_dr = 2 * NA_ROWS - 1
    rel = jnp.pad(na_bias, ((0, 0), (0, 0), (0, 0), (0, LANES - (2 * NA_COLS - 1))))
    rel = jnp.roll(rel, -(NA_COLS - 1), axis=-1)
    return pl.pallas_call(
        _na_bias_kernel,
        grid=(depth, N_HEADS),
        in_specs=[pl.BlockSpec((1, 1, n_dr, LANES), lambda l, h: (l, h, 0, 0))],
        out_specs=pl.BlockSpec((1, 3, 1, NA_QBLK, NA_KBLK), lambda l, h: (l, 0, h, 0, 0)),
        out_shape=jax.ShapeDtypeStruct((depth, 3, N_HEADS, NA_QBLK, NA_KBLK), F32),
        compiler_params=_params("arbitrary", "arbitrary"),
        name="na_bias",
    )(rel)


def _na_attn_kernel(q_ref, k_ref, v_ref, kc_ref, vc_ref, bias_ref, o_ref, *, n_blocks):
    blk = pl.program_id(1)
    start = pl.multiple_of(jnp.clip(blk - 1, 0, n_blocks - 3) * NA_QBLK, NA_QBLK)
    for h in range(N_HEADS):
        sl = slice(h * HEAD_DIM, (h + 1) * HEAD_DIM)
        q = q_ref[:, sl]
        kw = k_ref[pl.ds(start, NA_KBLK), sl]
        vw = v_ref[pl.ds(start, NA_KBLK), sl]
        s_loc = _dot_nt(q, kw) * ATT_SCALE + bias_ref[0, 0, h]
        s_ctx = _dot_nt(q, kc_ref[0, 0, :, sl].astype(BF16)) * ATT_SCALE
        mx = jnp.maximum(jnp.max(s_loc, axis=-1, keepdims=True),
                         jnp.max(s_ctx, axis=-1, keepdims=True))
        e_loc = jnp.exp(s_loc - mx)
        e_ctx = jnp.exp(s_ctx - mx)
        den = jnp.sum(e_loc, axis=-1, keepdims=True) + jnp.sum(e_ctx, axis=-1, keepdims=True)
        o = _dot(e_loc.astype(BF16), vw) + _dot(e_ctx.astype(BF16), vc_ref[0, 0, :, sl].astype(BF16))
        o_ref[:, sl] = (o / den).astype(BF16)


def _neighbourhood_attention(qn, kn, proj, k_ctx, v_ctx, bias, layer, *, seq):
    m = qn.shape[0]
    batch = m // seq
    n_blocks = seq // NA_QBLK
    past = k_ctx.shape[2]
    kind = lambda i: jnp.where(i == 0, 0, jnp.where(i == n_blocks - 1, 2, 1))
    ctx = pl.BlockSpec((1, 1, past, D_ATT), lambda b, i: (b, layer, 0, 0))
    return pl.pallas_call(
        functools.partial(_na_attn_kernel, n_blocks=n_blocks),
        grid=(batch, n_blocks),
        in_specs=[
            pl.BlockSpec((NA_QBLK, D_ATT), lambda b, i: (b * n_blocks + i, 0)),
            pl.BlockSpec((seq, D_ATT), lambda b, i: (b, 0)),
            pl.BlockSpec((seq, D_ATT), lambda b, i: (b, 2)),
            ctx, ctx,
            pl.BlockSpec((1, 1, N_HEADS, NA_QBLK, NA_KBLK), lambda b, i: (layer, kind(i), 0, 0, 0)),
        ],
        out_specs=pl.BlockSpec((NA_QBLK, D_ATT), lambda b, i: (b * n_blocks + i, 0)),
        out_shape=jax.ShapeDtypeStruct((m, D_ATT), BF16),
        compiler_params=_params("arbitrary", "arbitrary"),
        name="neighbourhood_attention",
    )(qn, kn, proj, k_ctx, v_ctx, bias)


def _stage_padded(pad_ref, x):
    zeros = jnp.zeros((PAD, pad_ref.shape[1]), F32)
    rows = x.shape[0]
    pad_ref[0:PAD, :] = zeros
    pad_ref[PAD + rows:2 * PAD + rows, :] = zeros
    pad_ref[PAD:PAD + rows, :] = x


def _conv_taps(pad_ref, seg, w, left, isolated):
    base = seg * SEG_LEN
    t = lax.broadcasted_iota(jnp.int32, (SEG_LEN, 1), 0)
    acc = None
    for k in range(w.shape[0]):
        off = k - left
        xs = pad_ref[PAD + base + off:PAD + base + off + SEG_LEN, :]
        if isolated and off != 0:
            xs = jnp.where((t + off >= 0) & (t + off < SEG_LEN), xs, 0.0)
        term = xs * w[k:k + 1, :]
        acc = term if acc is None else acc + term
    return acc


def _gelu_tanh(x):
    return x * (0.5 * (1.0 + jnp.tanh(0.7978845608028654 * (x + 0.044715 * (x * x * x)))))


def _lru_kernel(lx_ref, ly_ref, h0_ref, cw_ref, cb_ref, w_ref, b_ref, lam_ref, *rest, chained):
    if chained:
        y_ref, xpad, a_tm, u_tm, p_tm = rest
    else:
        y_ref, fin_ref, xpad, a_tm, u_tm = rest
    last = (SEG_LEN - 1) * SEG_PER_BLOCK
    _stage_padded(xpad, lx_ref[...].astype(F32))
    cw = cw_ref[0]
    neg_lam = -lam_ref[0]
    softplus = jnp.maximum(neg_lam, 0.0) + jnp.log1p(jnp.exp(-jnp.abs(neg_lam)))
    for c in range(SEG_PER_BLOCK):
        xc = cb_ref[0] + _conv_taps(xpad, c, cw, LRU_CONV // 2, not chained)
        seg_rows = pl.ds(c, SEG_LEN, stride=SEG_PER_BLOCK)
        for n in range(LRU_BLOCKS):
            cols = slice(n * LRU_BW, (n + 1) * LRU_BW)
            xb = xc[:, cols]
            g = _dot(xb.astype(BF16), w_ref[0, n].astype(BF16)) + b_ref[0, n:n + 1, :]
            for d in range(2):
                r = _sigmoid(g[:, (2 * d) * LRU_BW:(2 * d + 1) * LRU_BW])
                i = _sigmoid(g[:, (2 * d + 1) * LRU_BW:(2 * d + 2) * LRU_BW])
                log_a = -LRU_C * r * softplus[d:d + 1, cols]
                th = jnp.tanh(log_a)
                a_tm[d, n, seg_rows, :] = jnp.exp(log_a)
                u_tm[d, n, seg_rows, :] = jnp.sqrt(-2.0 * th / (1.0 - th)) * (i * xb)

    def scan_dir(d, rows, h, p):
        h_new, p_new = [], []
        for n in range(LRU_BLOCKS):
            a = a_tm[d, n, rows, :]
            hn = a * h[n] + u_tm[d, n, rows, :]
            u_tm[d, n, rows, :] = hn
            h_new.append(hn)
            if chained:
                pn = a * p[n]
                p_tm[d, n, rows, :] = pn
                p_new.append(pn)
        return tuple(h_new), tuple(p_new)

    def step(t, carry):
        hf, pf, hb, pb = carry
        hf, pf = scan_dir(0, pl.ds(pl.multiple_of(t * SEG_PER_BLOCK, SEG_PER_BLOCK), SEG_PER_BLOCK), hf, pf)
        tb = SEG_LEN - 1 - t
        hb, pb = scan_dir(1, pl.ds(pl.multiple_of(tb * SEG_PER_BLOCK, SEG_PER_BLOCK), SEG_PER_BLOCK), hb, pb)
        return hf, pf, hb, pb

    lane_blocks = [slice(n * LRU_BW, (n + 1) * LRU_BW) for n in range(LRU_BLOCKS)]
    if chained:
        zero = tuple(jnp.zeros((SEG_PER_BLOCK, LRU_BW), F32) for _ in lane_blocks)
        one = tuple(jnp.ones((SEG_PER_BLOCK, LRU_BW), F32) for _ in lane_blocks)
        init = (zero, one, zero, one)
    else:
        init = (tuple(h0_ref[0, :, cols] for cols in lane_blocks), (),
                tuple(h0_ref[1, :, cols] for cols in lane_blocks), ())
    lax.fori_loop(0, SEG_LEN, step, init, unroll=4)

    carry_f = carry_b = None
    if chained:
        carry_f, carry_b = [], []
        for n, cols in enumerate(lane_blocks):
            end_h, end_p = u_tm[0, n, last:last + SEG_PER_BLOCK, :], p_tm[0, n, last:last + SEG_PER_BLOCK, :]
            cf = [h0_ref[0, 0:1, cols]]
            for s in range(SEG_PER_BLOCK - 1):
                cf.append(end_h[s:s + 1] + end_p[s:s + 1] * cf[s])
            carry_f.append(cf)
            beg_h, beg_p = u_tm[1, n, 0:SEG_PER_BLOCK, :], p_tm[1, n, 0:SEG_PER_BLOCK, :]
            cb = [None] * SEG_PER_BLOCK
            cb[SEG_PER_BLOCK - 1] = h0_ref[0, 1:2, cols]
            for s in range(SEG_PER_BLOCK - 1, 0, -1):
                cb[s - 1] = beg_h[s:s + 1] + beg_p[s:s + 1] * cb[s]
            carry_b.append(cb)
    else:
        for n, cols in enumerate(lane_blocks):
            fin_ref[0, :, cols] = u_tm[0, n, last:last + SEG_PER_BLOCK, :]
            fin_ref[1, :, cols] = u_tm[1, n, 0:SEG_PER_BLOCK, :]

    for c in range(SEG_PER_BLOCK):
        seg_rows = pl.ds(c, SEG_LEN, stride=SEG_PER_BLOCK)
        tok = slice(c * SEG_LEN, (c + 1) * SEG_LEN)
        for n, cols in enumerate(lane_blocks):
            h = u_tm[0, n, seg_rows, :] + u_tm[1, n, seg_rows, :]
            if chained:
                h = h + p_tm[0, n, seg_rows, :] * carry_f[n][c] + p_tm[1, n, seg_rows, :] * carry_b[n][c]
            y_ref[tok, cols] = (h * _gelu_tanh(ly_ref[tok, cols].astype(F32))).astype(BF16)


def _rglru(proj, h0, conv_w, conv_b, w_cat, b_cat, lam, layer, *, chained):
    m = proj.shape[0]
    blocks = m // MIX_ROWS
    lx_col = 3 * D_ATT // D_LRU
    per_layer = lambda *shape: pl.BlockSpec((1,) + shape, lambda b: (layer,) + (0,) * len(shape))
    tm_buf = pltpu.VMEM((2, LRU_BLOCKS, MIX_ROWS, LRU_BW), F32)
    if chained:
        h0_spec = pl.BlockSpec((1, 2, D_LRU), lambda b: (b, 0, 0))
        out_specs = [pl.BlockSpec((MIX_ROWS, D_LRU), lambda b: (b, 0))]
        out_shape = [jax.ShapeDtypeStruct((m, D_LRU), BF16)]
        scratch = [tm_buf, tm_buf, tm_buf]
    else:
        h0_spec = pl.BlockSpec((2, SEG_PER_BLOCK, D_LRU), lambda b: (0, b, 0))
        out_specs = [pl.BlockSpec((MIX_ROWS, D_LRU), lambda b: (b, 0)),
                     pl.BlockSpec((2, SEG_PER_BLOCK, D_LRU), lambda b: (0, b, 0))]
        out_shape = [jax.ShapeDtypeStruct((m, D_LRU), BF16),
                     jax.ShapeDtypeStruct((2, blocks * SEG_PER_BLOCK, D_LRU), F32)]
        scratch = [tm_buf, tm_buf]
    return pl.pallas_call(
        functools.partial(_lru_kernel, chained=chained),
        grid=(blocks,),
        in_specs=[
            pl.BlockSpec((MIX_ROWS, D_LRU), lambda b: (b, lx_col)),
            pl.BlockSpec((MIX_ROWS, D_LRU), lambda b: (b, lx_col + 1)),
            h0_spec,
            per_layer(LRU_CONV, D_LRU), per_layer(1, D_LRU),
            per_layer(LRU_BLOCKS, LRU_BW, 4 * LRU_BW), per_layer(LRU_BLOCKS, 4 * LRU_BW),
            per_layer(2, D_LRU),
        ],
        out_specs=out_specs,
        out_shape=out_shape,
        scratch_shapes=[pltpu.VMEM((MIX_ROWS + 2 * PAD, D_LRU), F32)] + scratch,
        compiler_params=_params("arbitrary"),
        name="rglru",
    )(proj, proj, h0, conv_w, conv_b.reshape(conv_b.shape[0], 1, D_LRU), w_cat, b_cat, lam)


def _sconv_kernel(b_ref, c_ref, x_ref, w_ref, y_ref, pad, *, isolated):
    _stage_padded(pad, c_ref[...].astype(F32) * x_ref[...].astype(F32))
    w = w_ref[0]
    for c in range(SEG_PER_BLOCK):
        tok = slice(c * SEG_LEN, (c + 1) * SEG_LEN)
        conv = _conv_taps(pad, c, w, SC_CONV // 2, isolated)
        y_ref[tok, :] = (b_ref[tok, :].astype(F32) * conv).astype(BF16)


def _short_conv(proj, w, layer, *, isolated):
    m = proj.shape[0]
    col0 = (3 * D_ATT + 2 * D_LRU) // D_SC
    blk = lambda part: pl.BlockSpec((MIX_ROWS, D_SC), lambda b, part=part: (b, col0 + part))
    return pl.pallas_call(
        functools.partial(_sconv_kernel, isolated=isolated),
        grid=(m // MIX_ROWS,),
        in_specs=[blk(0), blk(1), blk(2), pl.BlockSpec((1, SC_CONV, D_SC), lambda b: (layer, 0, 0))],
        out_specs=pl.BlockSpec((MIX_ROWS, D_SC), lambda b: (b, 0)),
        out_shape=jax.ShapeDtypeStruct((m, D_SC), BF16),
        scratch_shapes=[pltpu.VMEM((MIX_ROWS + 2 * PAD, D_SC), F32)],
        compiler_params=_params("arbitrary"),
        name="short_conv",
    )(proj, proj, proj, w)


def _lru_gate_weights(w_a, b_a, w_i, b_i):
    depth = w_a.shape[0]
    w_cat = jnp.concatenate([w_a[:, 0], w_i[:, 0], w_a[:, 1], w_i[:, 1]], axis=-1)
    blocks = lambda b: b.reshape(depth, 2, LRU_BLOCKS, LRU_BW)
    ba, bi = blocks(b_a), blocks(b_i)
    b_cat = jnp.concatenate([ba[:, 0], bi[:, 0], ba[:, 1], bi[:, 1]], axis=-1)
    return w_cat, b_cat


def _stream_layer(h, mod, p, layer, *, rows_per_mod, ctx, caches):
    m = h.shape[0]
    proj = _norm_matmul(h, mod, p["g_mix"], [p["w_in"]], layer, shift_idx=0, scale_idx=1,
                        rows_per_mod=rows_per_mod, tn=512, name="in_proj")
    if ctx is None:
        k_acc, v_acc = caches
        att, k_acc, v_acc = _context_attention(proj, p["q_gain"], p["k_gain"], layer, k_acc, v_acc,
                                               seq=SEG_LEN)
        caches = (k_acc, v_acc)
        h0 = jnp.zeros((2, m // SEG_LEN, D_LRU), F32)
        lru, lru_final = _rglru(proj, h0, p["lru_conv_w"], p["lru_conv_b"], p["w_cat"], p["b_cat"],
                                p["lru_lambda"], layer, chained=False)
    else:
        k_ctx, v_ctx, h0, bias = ctx
        qn, kn = _qk_norm(proj, p["q_gain"], p["k_gain"], layer)
        att = _neighbourhood_attention(qn, kn, proj, k_ctx, v_ctx, bias, layer, seq=MIX_ROWS)
        (lru,) = _rglru(proj, h0, p["lru_conv_w"], p["lru_conv_b"], p["w_cat"], p["b_cat"],
                        p["lru_lambda"], layer, chained=True)
        lru_final = None
    conv = _short_conv(proj, p["sc_conv_w"], layer, isolated=ctx is None)
    h = _matmul_residual([att, lru, conv], p["w_out"], layer, h, mod, gate_idx=2,
                         rows_per_mod=rows_per_mod, tm=m, tn=256, ca=512, name="out_proj")
    hid = _norm_matmul(h, mod, p["g_ffn"], [p["w_ffn_gate"], p["w_ffn_up"]], layer, shift_idx=3,
                       scale_idx=4, rows_per_mod=rows_per_mod, tn=256, name="ffn_up")
    h = _matmul_residual([hid], p["w_ffn_down"], layer, h, mod, gate_idx=5,
                         rows_per_mod=rows_per_mod, tm=m // 2, tn=256, ca=256, name="ffn_down")
    return h, caches, lru_final


def kernel(x_prompt, x_sample, cache_k, cache_v, state_lru, c, c_ctx, w_mod, b_mod, g_mix, g_ffn, w_in, q_gain, k_gain, na_bias, lru_conv_w, lru_conv_b, lru_w_a, lru_b_a, lru_w_i, lru_b_i, lru_lambda, sc_conv_w, w_out, w_ffn_gate, w_ffn_up, w_ffn_down):
    batch, seq, d = x_prompt.shape
    dec_batch, dec_seq, _ = x_sample.shape
    depth = w_mod.shape[0]
    past = cache_k.shape[2]
    assert seq == SEG_LEN and dec_seq == MIX_ROWS and batch % SEG_PER_BLOCK == 0

    cond = jnp.zeros((MOD_ROWS, d), F32).at[0].set(c_ctx).at[1:1 + dec_batch].set(c)
    mod_all = _modulation(cond, w_mod, b_mod).reshape(depth, MOD_ROWS, 6, d)
    bias_all = _na_bias(na_bias)
    w_cat, b_cat = _lru_gate_weights(lru_w_a, lru_b_a, lru_w_i, lru_b_i)
    p = {
        "g_mix": g_mix, "g_ffn": g_ffn, "w_in": w_in, "q_gain": q_gain, "k_gain": k_gain,
        "lru_conv_w": lru_conv_w, "lru_conv_b": lru_conv_b, "w_cat": w_cat, "b_cat": b_cat,
        "lru_lambda": lru_lambda, "sc_conv_w": sc_conv_w, "w_out": w_out,
        "w_ffn_gate": w_ffn_gate, "w_ffn_up": w_ffn_up, "w_ffn_down": w_ffn_down,
    }
    k_ctx = cache_k.reshape(dec_batch, depth, past, D_ATT)
    v_ctx = cache_v.reshape(dec_batch, depth, past, D_ATT)

    hp = x_prompt.reshape(batch * seq, d)
    hs = x_sample.reshape(dec_batch * dec_seq, d)
    caches = (jnp.zeros((batch, depth, seq, D_ATT), F32), jnp.zeros((batch, depth, seq, D_ATT), F32))
    new_s = []
    for l in range(depth):
        hp, caches, s_l = _stream_layer(hp, mod_all[l, 0:1], p, l, rows_per_mod=batch * seq,
                                        ctx=None, caches=caches)
        new_s.append(jnp.swapaxes(s_l, 0, 1))
        ctx = (k_ctx, v_ctx, state_lru[:, l], bias_all)
        hs, _, _ = _stream_layer(hs, mod_all[l, 1:1 + dec_batch], p, l, rows_per_mod=dec_seq,
                                 ctx=ctx, caches=None)
    cache_shape = (batch, depth, seq, N_HEADS, HEAD_DIM)
    return (hp.reshape(batch, seq, d), hs.reshape(dec_batch, dec_seq, d),
            caches[0].reshape(cache_shape), caches[1].reshape(cache_shape), jnp.stack(new_s, axis=1))
```

```python
import functools

import jax
import jax.numpy as jnp
from jax import lax
from jax.experimental import pallas as pl
from jax.experimental.pallas import tpu as pltpu

D_MODEL = 2048
N_HEADS = 8
HEAD_DIM = 128
D_ATT = N_HEADS * HEAD_DIM
D_LRU = 512
LRU_BLOCKS = 4
LRU_BW = 128
LRU_CONV = 4
LRU_C = 8.0
D_SC = 512
SC_CONV = 3
GRID_W = 64
NA_ROWS = 8
NA_COLS = 16
EPS = 1e-6
ATT_SCALE = HEAD_DIM ** -0.5

NA_QROWS = 4
NA_KROWS = 12
NA_QBLK = NA_QROWS * GRID_W
NA_KBLK = NA_KROWS * GRID_W
MASK_VALUE = -1e30

LANES = 128
SUBLANES = 8
MOD_ROWS = SUBLANES
VMEM_LIMIT_BYTES = 56 * 1024 * 1024

SEG_LEN = 256
SEG_PER_BLOCK = SUBLANES
MIX_ROWS = SEG_LEN * SEG_PER_BLOCK
PAD = SUBLANES
MM_ROWS = 1024

F32 = jnp.float32
BF16 = jnp.bfloat16


def _params(*semantics):
    return pltpu.CompilerParams(dimension_semantics=semantics,
                                vmem_limit_bytes=VMEM_LIMIT_BYTES)


def _dot(a, b):
    return jnp.dot(a, b, preferred_element_type=F32)


def _dot_nt(a, b):
    return lax.dot_general(a, b, (((1,), (1,)), ((), ())), preferred_element_type=F32)


def _sigmoid(x):
    return 0.5 * (1.0 + jnp.tanh(0.5 * x))


def _mod_kernel(c_ref, w_ref, b_ref, o_ref):
    c = c_ref[...]
    s = (c * _sigmoid(c)).astype(BF16)
    o_ref[0] = _dot(s, w_ref[0].astype(BF16)) + b_ref[0]


def _modulation(cond, w_mod, b_mod, *, tn=1024):
    depth, d, n = w_mod.shape
    return pl.pallas_call(
        _mod_kernel,
        grid=(depth, n // tn),
        in_specs=[
            pl.BlockSpec((MOD_ROWS, d), lambda l, j: (0, 0)),
            pl.BlockSpec((1, d, tn), lambda l, j: (l, 0, j)),
            pl.BlockSpec((1, 1, tn), lambda l, j: (l, 0, j)),
        ],
        out_specs=pl.BlockSpec((1, MOD_ROWS, tn), lambda l, j: (l, 0, j)),
        out_shape=jax.ShapeDtypeStruct((depth, MOD_ROWS, n), F32),
        compiler_params=_params("arbitrary", "arbitrary"),
        name="modulation",
    )(cond, w_mod, b_mod.reshape(depth, 1, n))


def _norm_mm_kernel(x_ref, mod_ref, g_ref, *rest, n_w, shift_idx, scale_idx, n_chunks):
    w_refs, o_ref, u_ref, wb_refs = rest[:n_w], rest[n_w], rest[n_w + 1], rest[n_w + 2:]
    s = pl.program_id(0)
    cm = x_ref.shape[0]
    slot = jnp.maximum(s - n_chunks, 0) % 2

    def cast_weights(into):
        for w_ref, wb_ref in zip(w_refs, wb_refs):
            wb_ref[into] = w_ref[0].astype(BF16)

    def matmul_rows(row0, rows, use):
        sl = pl.ds(pl.multiple_of(row0, rows), rows)
        u = u_ref[sl, :]
        a = _dot(u, wb_refs[0][use])
        if n_w == 2:
            a = a * _sigmoid(a) * _dot(u, wb_refs[1][use])
        o_ref[sl, :] = a.astype(o_ref.dtype)

    def norm_chunk():
        gain = g_ref[0] * (1.0 + mod_ref[0, scale_idx:scale_idx + 1, :])
        shift = mod_ref[0, shift_idx:shift_idx + 1, :]
        for p in range(cm // LANES):
            x = x_ref[p * LANES:(p + 1) * LANES, :]
            y = x * lax.rsqrt(jnp.mean(x * x, axis=-1, keepdims=True) + EPS)
            u_ref[pl.ds(pl.multiple_of(s * cm + p * LANES, LANES), LANES), :] = (y * gain + shift).astype(BF16)

    @pl.when(s == 0)
    def _():
        cast_weights(0)
        norm_chunk()

    @pl.when((s >= 1) & (s < n_chunks))
    def _():
        matmul_rows((s - 1) * cm, cm, 0)
        norm_chunk()

    @pl.when(s == n_chunks)
    def _():
        matmul_rows((n_chunks - 1) * cm, cm, 0)
        cast_weights(1)

    @pl.when(s > n_chunks)
    def _():
        for r in range(u_ref.shape[0] // MM_ROWS):
            matmul_rows(r * MM_ROWS, MM_ROWS, slot)
        cast_weights(1 - slot)


def _norm_matmul(x, mod, gain, weights, layer, *, shift_idx, scale_idx, rows_per_mod, tn, name,
                 cm=512):
    m, d = x.shape
    n = weights[0].shape[2]
    n_chunks = m // cm
    n_tiles = n // tn
    chunks_per_mod = rows_per_mod // cm
    chunk = lambda s: jnp.minimum(s, n_chunks - 1)
    tile = lambda s: jnp.maximum(s - n_chunks, 0)
    next_tile = lambda s: jnp.minimum(tile(s + 1), n_tiles - 1)
    kernel = functools.partial(_norm_mm_kernel, n_w=len(weights), shift_idx=shift_idx,
                               scale_idx=scale_idx, n_chunks=n_chunks)
    return pl.pallas_call(
        kernel,
        grid=(n_chunks + n_tiles,),
        in_specs=[
            pl.BlockSpec((cm, d), lambda s: (chunk(s), 0)),
            pl.BlockSpec((1, 6, d), lambda s: (chunk(s) // chunks_per_mod, 0, 0)),
            pl.BlockSpec((1, 1, d), lambda s: (layer, 0, 0)),
        ] + [pl.BlockSpec((1, d, tn), lambda s: (layer, 0, next_tile(s))) for _ in weights],
        out_specs=pl.BlockSpec((m, tn), lambda s: (0, tile(s))),
        out_shape=jax.ShapeDtypeStruct((m, n), BF16),
        scratch_shapes=[pltpu.VMEM((m, d), BF16)] + [pltpu.VMEM((2, d, tn), BF16) for _ in weights],
        compiler_params=_params("arbitrary"),
        name=name,
    )(x, mod, gain.reshape(gain.shape[0], 1, d), *weights)


def _mm_res_kernel(*refs, splits, gate_idx, rows_per_mod, n_chunks):
    n = len(splits)
    a_refs = refs[:n]
    w_ref, h_ref, mod_ref, o_ref, a_res, wb_ref = refs[n:]
    s = pl.program_id(1)
    tm = h_ref.shape[0]
    ca = a_refs[0].shape[0]
    tile_row0 = pl.program_id(0) * tm
    slot = jnp.maximum(s - n_chunks + 1, 0) % 2

    def cast_weights(into):
        wb_ref[into] = w_ref[0].astype(BF16)

    def product(row0, rows, use):
        sl = pl.ds(pl.multiple_of(row0, rows), rows)
        gate = mod_ref[(tile_row0 + row0) // rows_per_mod, gate_idx:gate_idx + 1, :]
        o_ref[sl, :] = h_ref[sl, :] + gate * _dot(a_res[sl, :], wb_ref[use])

    def stage_chunk():
        rows = pl.ds(pl.multiple_of(s * ca, ca), ca)
        k0 = 0
        for a_ref, kk in zip(a_refs, splits):
            a_res[rows, k0:k0 + kk] = a_ref[...]
            k0 += kk
        product(s * ca, ca, 0)

    @pl.when(s == 0)
    def _():
        cast_weights(0)
        stage_chunk()

    @pl.when((s >= 1) & (s < n_chunks - 1))
    def _():
        stage_chunk()

    @pl.when(s == n_chunks - 1)
    def _():
        stage_chunk()
        cast_weights(1)

    @pl.when(s >= n_chunks)
    def _():
        for r in range(tm // MM_ROWS):
            product(r * MM_ROWS, MM_ROWS, slot)
        cast_weights(1 - slot)


def _matmul_residual(a_parts, w, layer, h, mod, *, gate_idx, rows_per_mod, tm, tn, ca, name):
    m, n = h.shape
    k = w.shape[1]
    splits = tuple(a.shape[1] for a in a_parts)
    n_chunks = tm // ca
    n_tiles = n // tn
    assert n_chunks >= 2
    chunk = lambda i, s: i * n_chunks + jnp.minimum(s, n_chunks - 1)
    tile = lambda s: jnp.maximum(s - n_chunks + 1, 0)
    next_tile = lambda s: jnp.minimum(tile(s + 1), n_tiles - 1)
    kernel = functools.partial(_mm_res_kernel, splits=splits, gate_idx=gate_idx,
                               rows_per_mod=rows_per_mod, n_chunks=n_chunks)
    return pl.pallas_call(
        kernel,
        grid=(m // tm, n_chunks + n_tiles - 1),
        in_specs=[pl.BlockSpec((ca, kk), lambda i, s: (chunk(i, s), 0)) for kk in splits] + [
            pl.BlockSpec((1, k, tn), lambda i, s: (layer, 0, next_tile(s))),
            pl.BlockSpec((tm, tn), lambda i, s: (i, tile(s))),
            pl.BlockSpec((mod.shape[0], 6, tn), lambda i, s: (0, 0, tile(s))),
        ],
        out_specs=pl.BlockSpec((tm, tn), lambda i, s: (i, tile(s))),
        out_shape=jax.ShapeDtypeStruct((m, n), F32),
        scratch_shapes=[pltpu.VMEM((tm, k), BF16), pltpu.VMEM((2, k, tn), BF16)],
        compiler_params=_params("arbitrary", "arbitrary"),
        name=name,
    )(*a_parts, w, h, mod)


def _head_rms(x, gain):
    return x * lax.rsqrt(jnp.mean(x * x, axis=-1, keepdims=True) + EPS) * gain


def _ctx_attn_kernel(q_ref, k_ref, v_ref, qg_ref, kg_ref, kacc_ref, vacc_ref,
                     att_ref, kout_ref, vout_ref):
    del kacc_ref, vacc_ref
    qg = qg_ref[0]
    kg = kg_ref[0]
    for h in range(N_HEADS):
        sl = slice(h * HEAD_DIM, (h + 1) * HEAD_DIM)
        qn = _head_rms(q_ref[:, sl].astype(F32), qg)
        kn = _head_rms(k_ref[:, sl].astype(F32), kg)
        v = v_ref[:, sl]
        kout_ref[0, 0, :, sl] = kn
        vout_ref[0, 0, :, sl] = v.astype(F32)
        s = _dot_nt(qn.astype(BF16), kn.astype(BF16)) * ATT_SCALE
        e = jnp.exp(s - jnp.max(s, axis=-1, keepdims=True))
        den = jnp.sum(e, axis=-1, keepdims=True)
        att_ref[:, sl] = (_dot(e.astype(BF16), v) / den).astype(BF16)


def _context_attention(proj, q_gain, k_gain, layer, k_acc, v_acc, *, seq):
    m = proj.shape[0]
    blk = lambda c: pl.BlockSpec((seq, D_ATT), lambda b, c=c: (b, c))
    gain = pl.BlockSpec((1, 1, HEAD_DIM), lambda b: (layer, 0, 0))
    cache = pl.BlockSpec((1, 1, seq, D_ATT), lambda b: (b, layer, 0, 0))
    hbm = pl.BlockSpec(memory_space=pl.ANY)
    gains = lambda g: g.reshape(g.shape[0], 1, HEAD_DIM)
    return pl.pallas_call(
        _ctx_attn_kernel,
        grid=(m // seq,),
        in_specs=[blk(0), blk(1), blk(2), gain, gain, hbm, hbm],
        out_specs=[pl.BlockSpec((seq, D_ATT), lambda b: (b, 0)), cache, cache],
        out_shape=[jax.ShapeDtypeStruct((m, D_ATT), BF16),
                   jax.ShapeDtypeStruct(k_acc.shape, F32),
                   jax.ShapeDtypeStruct(v_acc.shape, F32)],
        input_output_aliases={5: 1, 6: 2},
        compiler_params=_params("arbitrary"),
        name="context_attention",
    )(proj, proj, proj, gains(q_gain), gains(k_gain), k_acc, v_acc)


def _qk_norm_kernel(q_ref, k_ref, qg_ref, kg_ref, qo_ref, ko_ref):
    qg = qg_ref[0]
    kg = kg_ref[0]
    for h in range(N_HEADS):
        sl = slice(h * HEAD_DIM, (h + 1) * HEAD_DIM)
        qo_ref[:, sl] = _head_rms(q_ref[:, sl].astype(F32), qg).astype(BF16)
        ko_ref[:, sl] = _head_rms(k_ref[:, sl].astype(F32), kg).astype(BF16)


def _qk_norm(proj, q_gain, k_gain, layer, *, tm=512):
    m = proj.shape[0]
    blk = lambda c: pl.BlockSpec((tm, D_ATT), lambda i, c=c: (i, c))
    gain = pl.BlockSpec((1, 1, HEAD_DIM), lambda i: (layer, 0, 0))
    out = pl.BlockSpec((tm, D_ATT), lambda i: (i, 0))
    gains = lambda g: g.reshape(g.shape[0], 1, HEAD_DIM)
    return pl.pallas_call(
        _qk_norm_kernel,
        grid=(m // tm,),
        in_specs=[blk(0), blk(1), gain, gain],
        out_specs=[out, out],
        out_shape=[jax.ShapeDtypeStruct((m, D_ATT), BF16)] * 2,
        compiler_params=_params("arbitrary"),
        name="qk_norm",
    )(proj, proj, gains(q_gain), gains(k_gain))


def _na_block_geometry():
    rows = 32
    return ((0, 0), (NA_QROWS, 0), (rows - NA_QROWS, rows - NA_KROWS)), rows


def _na_bias_kernel(rel_ref, o_ref):
    qc = lax.broadcasted_iota(jnp.int32, (GRID_W, GRID_W), 0)
    kc = lax.broadcasted_iota(jnp.int32, (GRID_W, GRID_W), 1)
    cs = jnp.clip(qc - NA_COLS // 2, 0, GRID_W - NA_COLS)
    col_ok = (kc >= cs) & (kc < cs + NA_COLS)
    dc = kc - qc + (NA_COLS - 1)
    hits = [dc == j for j in range(2 * NA_COLS - 1)]
    masked = jnp.full((GRID_W, GRID_W), MASK_VALUE, F32)
    tiles = []
    for dr in range(2 * NA_ROWS - 1):
        x = masked
        for j, hit in enumerate(hits):
            x = jnp.where(hit, rel_ref[0, 0, dr, j], x)
        tiles.append(jnp.where(col_ok, x, MASK_VALUE))
    kinds, rows = _na_block_geometry()
    for kind, (r0, kb) in enumerate(kinds):
        for qr in range(NA_QROWS):
            r = r0 + qr
            rs = min(max(r - NA_ROWS // 2, 0), rows - NA_ROWS)
            for kr in range(NA_KROWS):
                krow = kb + kr
                tile = tiles[krow - r + NA_ROWS - 1] if rs <= krow < rs + NA_ROWS else masked
                o_ref[0, kind, 0, qr * GRID_W:(qr + 1) * GRID_W,
                      kr * GRID_W:(kr + 1) * GRID_W] = tile


def _na_bias(na_bias):
    depth = na_bias.shape[0]
    return pl.pallas_call(
        _na_bias_kernel,
        grid=(depth, N_HEADS),
        in_specs=[pl.BlockSpec((1, 1) + na_bias.shape[2:], lambda l, h: (l, h, 0, 0),
                               memory_space=pltpu.SMEM)],
        out_specs=pl.BlockSpec((1, 3, 1, NA_QBLK, NA_KBLK), lambda l, h: (l, 0, h, 0, 0)),
        out_shape=jax.ShapeDtypeStruct((depth, 3, N_HEADS, NA_QBLK, NA_KBLK), F32),
        compiler_params=_params("arbitrary", "arbitrary"),
        name="na_bias",
    )(na_bias)


def _na_attn_kernel(q_ref, k_ref, v_ref, kc_ref, vc_ref, bias_ref, o_ref, *, n_blocks):
    blk = pl.program_id(1)
    start = pl.multiple_of(jnp.clip(blk - 1, 0, n_blocks - 3) * NA_QBLK, NA_QBLK)
    for h in range(N_HEADS):
        sl = slice(h * HEAD_DIM, (h + 1) * HEAD_DIM)
        q = q_ref[:, sl]
        kw = k_ref[pl.ds(start, NA_KBLK), sl]
        vw = v_ref[pl.ds(start, NA_KBLK), sl]
        s_loc = _dot_nt(q, kw) * ATT_SCALE + bias_ref[0, 0, h]
        s_ctx = _dot_nt(q, kc_ref[0, 0, :, sl].astype(BF16)) * ATT_SCALE
        mx = jnp.maximum(jnp.max(s_loc, axis=-1, keepdims=True),
                         jnp.max(s_ctx, axis=-1, keepdims=True))
        e_loc = jnp.exp(s_loc - mx)
        e_ctx = jnp.exp(s_ctx - mx)
        den = jnp.sum(e_loc, axis=-1, keepdims=True) + jnp.sum(e_ctx, axis=-1, keepdims=True)
        o = _dot(e_loc.astype(BF16), vw) + _dot(e_ctx.astype(BF16), vc_ref[0, 0, :, sl].astype(BF16))
        o_ref[:, sl] = (o / den).astype(BF16)


def _neighbourhood_attention(qn, kn, proj, k_ctx, v_ctx, bias, layer, *, seq):
    m = qn.shape[0]
    batch = m // seq
    n_blocks = seq // NA_QBLK
    past = k_ctx.shape[2]
    kind = lambda i: jnp.where(i == 0, 0, jnp.where(i == n_blocks - 1, 2, 1))
    ctx = pl.BlockSpec((1, 1, past, D_ATT), lambda b, i: (b, layer, 0, 0))
    return pl.pallas_call(
        functools.partial(_na_attn_kernel, n_blocks=n_blocks),
        grid=(batch, n_blocks),
        in_specs=[
            pl.BlockSpec((NA_QBLK, D_ATT), lambda b, i: (b * n_blocks + i, 0)),
            pl.BlockSpec((seq, D_ATT), lambda b, i: (b, 0)),
            pl.BlockSpec((seq, D_ATT), lambda b, i: (b, 2)),
            ctx, ctx,
            pl.BlockSpec((1, 1, N_HEADS, NA_QBLK, NA_KBLK), lambda b, i: (layer, kind(i), 0, 0, 0)),
        ],
        out_specs=pl.BlockSpec((NA_QBLK, D_ATT), lambda b, i: (b * n_blocks + i, 0)),
        out_shape=jax.ShapeDtypeStruct((m, D_ATT), BF16),
        compiler_params=_params("arbitrary", "arbitrary"),
        name="neighbourhood_attention",
    )(qn, kn, proj, k_ctx, v_ctx, bias)


def _stage_padded(pad_ref, x):
    zeros = jnp.zeros((PAD, pad_ref.shape[1]), F32)
    rows = x.shape[0]
    pad_ref[0:PAD, :] = zeros
    pad_ref[PAD + rows:2 * PAD + rows, :] = zeros
    pad_ref[PAD:PAD + rows, :] = x


def _conv_taps(pad_ref, seg, w, left, isolated):
    base = seg * SEG_LEN
    t = lax.broadcasted_iota(jnp.int32, (SEG_LEN, 1), 0)
    acc = None
    for k in range(w.shape[0]):
        off = k - left
        xs = pad_ref[PAD + base + off:PAD + base + off + SEG_LEN, :]
        if isolated and off != 0:
            xs = jnp.where((t + off >= 0) & (t + off < SEG_LEN), xs, 0.0)
        term = xs * w[k:k + 1, :]
        acc = term if acc is None else acc + term
    return acc


def _gelu_tanh(x):
    return x * (0.5 * (1.0 + jnp.tanh(0.7978845608028654 * (x + 0.044715 * (x * x * x)))))


def _lru_kernel(lxy_ref, h0_ref, cw_ref, cb_ref, w_ref, b_ref, lam_ref, *rest, chained):
    if chained:
        y_ref, xpad, a_tm, u_tm, p_tm = rest
    else:
        y_ref, fin_ref, xpad, a_tm, u_tm = rest
    last = (SEG_LEN - 1) * SEG_PER_BLOCK
    _stage_padded(xpad, lxy_ref[:, 0:D_LRU].astype(F32))
    cw = cw_ref[0]
    neg_lam = -lam_ref[0]
    softplus = jnp.maximum(neg_lam, 0.0) + jnp.log1p(jnp.exp(-jnp.abs(neg_lam)))
    decay = (-0.5 * LRU_C) * softplus
    for c in range(SEG_PER_BLOCK):
        xc = cb_ref[0] + _conv_taps(xpad, c, cw, LRU_CONV // 2, not chained)
        seg_rows = pl.ds(c, SEG_LEN, stride=SEG_PER_BLOCK)
        for n in range(LRU_BLOCKS):
            cols = slice(n * LRU_BW, (n + 1) * LRU_BW)
            xb = xc[:, cols]
            g = _dot(xb.astype(BF16), w_ref[0, n].astype(BF16)) + b_ref[0, n:n + 1, :]
            x_half = 0.5 * xb
            for d in range(2):
                ta = jnp.tanh(g[:, (2 * d) * LRU_BW:(2 * d + 1) * LRU_BW])
                ti = jnp.tanh(g[:, (2 * d + 1) * LRU_BW:(2 * d + 2) * LRU_BW])
                log_a = decay[d:d + 1, cols] * (1.0 + ta)
                th = jnp.tanh(log_a)
                a_tm[d, n, seg_rows, :] = jnp.exp(log_a)
                u_tm[d, n, seg_rows, :] = (jnp.sqrt(-2.0 * th) * lax.rsqrt(1.0 - th)) * ((1.0 + ti) * x_half)

    def scan_dir(d, rows, h, p):
        h_new, p_new = [], []
        for n in range(LRU_BLOCKS):
            a = a_tm[d, n, rows, :]
            hn = a * h[n] + u_tm[d, n, rows, :]
            u_tm[d, n, rows, :] = hn
            h_new.append(hn)
            if chained:
                pn = a * p[n]
                p_tm[d, n, rows, :] = pn
                p_new.append(pn)
        return tuple(h_new), tuple(p_new)

    def step(t, carry):
        hf, pf, hb, pb = carry
        hf, pf = scan_dir(0, pl.ds(pl.multiple_of(t * SEG_PER_BLOCK, SEG_PER_BLOCK), SEG_PER_BLOCK), hf, pf)
        tb = SEG_LEN - 1 - t
        hb, pb = scan_dir(1, pl.ds(pl.multiple_of(tb * SEG_PER_BLOCK, SEG_PER_BLOCK), SEG_PER_BLOCK), hb, pb)
        return hf, pf, hb, pb

    lane_blocks = [slice(n * LRU_BW, (n + 1) * LRU_BW) for n in range(LRU_BLOCKS)]
    if chained:
        zero = tuple(jnp.zeros((SEG_PER_BLOCK, LRU_BW), F32) for _ in lane_blocks)
        one = tuple(jnp.ones((SEG_PER_BLOCK, LRU_BW), F32) for _ in lane_blocks)
        init = (zero, one, zero, one)
    else:
        init = (tuple(h0_ref[0, :, cols] for cols in lane_blocks), (),
                tuple(h0_ref[1, :, cols] for cols in lane_blocks), ())
    lax.fori_loop(0, SEG_LEN, step, init, unroll=4)

    carry_f = carry_b = None
    if chained:
        carry_f, carry_b = [], []
        for n, cols in enumerate(lane_blocks):
            end_h, end_p = u_tm[0, n, last:last + SEG_PER_BLOCK, :], p_tm[0, n, last:last + SEG_PER_BLOCK, :]
            cf = [h0_ref[0, 0:1, cols]]
            for s in range(SEG_PER_BLOCK - 1):
                cf.append(end_h[s:s + 1] + end_p[s:s + 1] * cf[s])
            carry_f.append(cf)
            beg_h, beg_p = u_tm[1, n, 0:SEG_PER_BLOCK, :], p_tm[1, n, 0:SEG_PER_BLOCK, :]
            cb = [None] * SEG_PER_BLOCK
            cb[SEG_PER_BLOCK - 1] = h0_ref[0, 1:2, cols]
            for s in range(SEG_PER_BLOCK - 1, 0, -1):
                cb[s - 1] = beg_h[s:s + 1] + beg_p[s:s + 1] * cb[s]
            carry_b.append(cb)
    else:
        for n, cols in enumerate(lane_blocks):
            fin_ref[0, :, cols] = u_tm[0, n, last:last + SEG_PER_BLOCK, :]
            fin_ref[1, :, cols] = u_tm[1, n, 0:SEG_PER_BLOCK, :]

    for c in range(SEG_PER_BLOCK):
        seg_rows = pl.ds(c, SEG_LEN, stride=SEG_PER_BLOCK)
        tok = slice(c * SEG_LEN, (c + 1) * SEG_LEN)
        for n, cols in enumerate(lane_blocks):
            h = u_tm[0, n, seg_rows, :] + u_tm[1, n, seg_rows, :]
            if chained:
                h = h + p_tm[0, n, seg_rows, :] * carry_f[n][c] + p_tm[1, n, seg_rows, :] * carry_b[n][c]
            ly = lxy_ref[tok, D_LRU + n * LRU_BW:D_LRU + (n + 1) * LRU_BW].astype(F32)
            y_ref[tok, cols] = (h * _gelu_tanh(ly)).astype(BF16)


def _rglru(proj, h0, conv_w, conv_b, w_cat, b_cat, lam, layer, *, chained):
    m = proj.shape[0]
    blocks = m // MIX_ROWS
    lxy_col = 3 * D_ATT // (2 * D_LRU)
    per_layer = lambda *shape: pl.BlockSpec((1,) + shape, lambda b: (layer,) + (0,) * len(shape))
    tm_buf = pltpu.VMEM((2, LRU_BLOCKS, MIX_ROWS, LRU_BW), F32)
    if chained:
        h0_spec = pl.BlockSpec((1, 2, D_LRU), lambda b: (b, 0, 0))
        out_specs = [pl.BlockSpec((MIX_ROWS, D_LRU), lambda b: (b, 0))]
        out_shape = [jax.ShapeDtypeStruct((m, D_LRU), BF16)]
        scratch = [tm_buf, tm_buf, tm_buf]
    else:
        h0_spec = pl.BlockSpec((2, SEG_PER_BLOCK, D_LRU), lambda b: (0, b, 0))
        out_specs = [pl.BlockSpec((MIX_ROWS, D_LRU), lambda b: (b, 0)),
                     pl.BlockSpec((2, SEG_PER_BLOCK, D_LRU), lambda b: (0, b, 0))]
        out_shape = [jax.ShapeDtypeStruct((m, D_LRU), BF16),
                     jax.ShapeDtypeStruct((2, blocks * SEG_PER_BLOCK, D_LRU), F32)]
        scratch = [tm_buf, tm_buf]
    return pl.pallas_call(
        functools.partial(_lru_kernel, chained=chained),
        grid=(blocks,),
        in_specs=[
            pl.BlockSpec((MIX_ROWS, 2 * D_LRU), lambda b: (b, lxy_col)),
            h0_spec,
            per_layer(LRU_CONV, D_LRU), per_layer(1, D_LRU),
            per_layer(LRU_BLOCKS, LRU_BW, 4 * LRU_BW), per_layer(LRU_BLOCKS, 4 * LRU_BW),
            per_layer(2, D_LRU),
        ],
        out_specs=out_specs,
        out_shape=out_shape,
        scratch_shapes=[pltpu.VMEM((MIX_ROWS + 2 * PAD, D_LRU), F32)] + scratch,
        compiler_params=_params("arbitrary"),
        name="rglru",
    )(proj, h0, conv_w, conv_b.reshape(conv_b.shape[0], 1, D_LRU), w_cat, b_cat, lam)


def _sconv_kernel(b_ref, c_ref, x_ref, w_ref, y_ref, pad, *, isolated):
    _stage_padded(pad, c_ref[...].astype(F32) * x_ref[...].astype(F32))
    w = w_ref[0]
    for c in range(SEG_PER_BLOCK):
        tok = slice(c * SEG_LEN, (c + 1) * SEG_LEN)
        conv = _conv_taps(pad, c, w, SC_CONV // 2, isolated)
        y_ref[tok, :] = (b_ref[tok, :].astype(F32) * conv).astype(BF16)


def _short_conv(proj, w, layer, *, isolated):
    m = proj.shape[0]
    col0 = (3 * D_ATT + 2 * D_LRU) // D_SC
    blk = lambda part: pl.BlockSpec((MIX_ROWS, D_SC), lambda b, part=part: (b, col0 + part))
    return pl.pallas_call(
        functools.partial(_sconv_kernel, isolated=isolated),
        grid=(m // MIX_ROWS,),
        in_specs=[blk(0), blk(1), blk(2), pl.BlockSpec((1, SC_CONV, D_SC), lambda b: (layer, 0, 0))],
        out_specs=pl.BlockSpec((MIX_ROWS, D_SC), lambda b: (b, 0)),
        out_shape=jax.ShapeDtypeStruct((m, D_SC), BF16),
        scratch_shapes=[pltpu.VMEM((MIX_ROWS + 2 * PAD, D_SC), F32)],
        compiler_params=_params("arbitrary"),
        name="short_conv",
    )(proj, proj, proj, w)


def _lru_gate_weights(w_a, b_a, w_i, b_i):
    depth = w_a.shape[0]
    w_cat = jnp.concatenate([w_a[:, 0], w_i[:, 0], w_a[:, 1], w_i[:, 1]], axis=-1)
    blocks = lambda b: b.reshape(depth, 2, LRU_BLOCKS, LRU_BW)
    ba, bi = blocks(b_a), blocks(b_i)
    b_cat = jnp.concatenate([ba[:, 0], bi[:, 0], ba[:, 1], bi[:, 1]], axis=-1)
    return 0.5 * w_cat, 0.5 * b_cat


def _stream_layer(h, mod, p, layer, *, rows_per_mod, ctx, caches):
    m = h.shape[0]
    proj = _norm_matmul(h, mod, p["g_mix"], [p["w_in"]], layer, shift_idx=0, scale_idx=1,
                        rows_per_mod=rows_per_mod, tn=512, name="in_proj")
    if ctx is None:
        k_acc, v_acc = caches
        att, k_acc, v_acc = _context_attention(proj, p["q_gain"], p["k_gain"], layer, k_acc, v_acc,
                                               seq=SEG_LEN)
        caches = (k_acc, v_acc)
        h0 = jnp.zeros((2, m // SEG_LEN, D_LRU), F32)
        lru, lru_final = _rglru(proj, h0, p["lru_conv_w"], p["lru_conv_b"], p["w_cat"], p["b_cat"],
                                p["lru_lambda"], layer, chained=False)
    else:
        k_ctx, v_ctx, h0, bias = ctx
        qn, kn = _qk_norm(proj, p["q_gain"], p["k_gain"], layer)
        att = _neighbourhood_attention(qn, kn, proj, k_ctx, v_ctx, bias, layer, seq=MIX_ROWS)
        (lru,) = _rglru(proj, h0, p["lru_conv_w"], p["lru_conv_b"], p["w_cat"], p["b_cat"],
                        p["lru_lambda"], layer, chained=True)
        lru_final = None
    conv = _short_conv(proj, p["sc_conv_w"], layer, isolated=ctx is None)
    h = _matmul_residual([att, lru, conv], p["w_out"], layer, h, mod, gate_idx=2,
                         rows_per_mod=rows_per_mod, tm=m, tn=256, ca=512, name="out_proj")
    hid = _norm_matmul(h, mod, p["g_ffn"], [p["w_ffn_gate"], p["w_ffn_up"]], layer, shift_idx=3,
                       scale_idx=4, rows_per_mod=rows_per_mod, tn=256, name="ffn_up")
    h = _matmul_residual([hid], p["w_ffn_down"], layer, h, mod, gate_idx=5,
                         rows_per_mod=rows_per_mod, tm=m // 2, tn=256, ca=128, name="ffn_down")
    return h, caches, lru_final


def kernel(x_prompt, x_sample, cache_k, cache_v, state_lru, c, c_ctx, w_mod, b_mod, g_mix, g_ffn, w_in, q_gain, k_gain, na_bias, lru_conv_w, lru_conv_b, lru_w_a, lru_b_a, lru_w_i, lru_b_i, lru_lambda, sc_conv_w, w_out, w_ffn_gate, w_ffn_up, w_ffn_down):
    batch, seq, d = x_prompt.shape
    dec_batch, dec_seq, _ = x_sample.shape
    depth = w_mod.shape[0]
    past = cache_k.shape[2]
    assert seq == SEG_LEN and dec_seq == MIX_ROWS and batch % SEG_PER_BLOCK == 0

    cond = jnp.zeros((MOD_ROWS, d), F32).at[0].set(c_ctx).at[1:1 + dec_batch].set(c)
    mod_all = _modulation(cond, w_mod, b_mod).reshape(depth, MOD_ROWS, 6, d)
    bias_all = _na_bias(na_bias)
    w_cat, b_cat = _lru_gate_weights(lru_w_a, lru_b_a, lru_w_i, lru_b_i)
    p = {
        "g_mix": g_mix, "g_ffn": g_ffn, "w_in": w_in, "q_gain": q_gain, "k_gain": k_gain,
        "lru_conv_w": lru_conv_w, "lru_conv_b": lru_conv_b, "w_cat": w_cat, "b_cat": b_cat,
        "lru_lambda": lru_lambda, "sc_conv_w": sc_conv_w, "w_out": w_out,
        "w_ffn_gate": w_ffn_gate, "w_ffn_up": w_ffn_up, "w_ffn_down": w_ffn_down,
    }
    k_ctx = cache_k.reshape(dec_batch, depth, past, D_ATT)
    v_ctx = cache_v.reshape(dec_batch, depth, past, D_ATT)

    hp = x_prompt.reshape(batch * seq, d)
    hs = x_sample.reshape(dec_batch * dec_seq, d)
    caches = (jnp.zeros((batch, depth, seq, D_ATT), F32), jnp.zeros((batch, depth, seq, D_ATT), F32))
    new_s = []
    for l in range(depth):
        hp, caches, s_l = _stream_layer(hp, mod_all[l, 0:1], p, l, rows_per_mod=batch * seq,
                                        ctx=None, caches=caches)
        new_s.append(jnp.swapaxes(s_l, 0, 1))
        ctx = (k_ctx, v_ctx, state_lru[:, l], bias_all)
        hs, _, _ = _stream_layer(hs, mod_all[l, 1:1 + dec_batch], p, l, rows_per_mod=dec_seq,
                                 ctx=ctx, caches=None)
    cache_shape = (batch, depth, seq, N_HEADS, HEAD_DIM)
    return (hp.reshape(batch, seq, d), hs.reshape(dec_batch, dec_seq, d),
            caches[0].reshape(cache_shape), caches[1].reshape(cache_shape), jnp.stack(new_s, axis=1))
```

```python
import functools

import jax
import jax.numpy as jnp
from jax import lax
from jax.experimental import pallas as pl
from jax.experimental.pallas import tpu as pltpu

D_MODEL = 2048
N_HEADS = 8
HEAD_DIM = 128
D_ATT = N_HEADS * HEAD_DIM
D_LRU = 512
LRU_BLOCKS = 4
LRU_BW = 128
LRU_CONV = 4
LRU_C = 8.0
D_SC = 512
SC_CONV = 3
GRID_W = 64
NA_ROWS = 8
NA_COLS = 16
EPS = 1e-6
LOG2_E = 1.4426950408889634
LOGIT_SCALE = HEAD_DIM ** -0.5 * LOG2_E

NA_QROWS = 4
NA_KROWS = 12
NA_QBLK = NA_QROWS * GRID_W
NA_KBLK = NA_KROWS * GRID_W
MASK_VALUE = -1e30

LANES = 128
SUBLANES = 8
MOD_ROWS = SUBLANES
VMEM_LIMIT_BYTES = 60 * 1024 * 1024

SEG_LEN = 256
SEG_PER_BLOCK = SUBLANES
MIX_ROWS = SEG_LEN * SEG_PER_BLOCK
PAD = SUBLANES
MM_ROWS = 1024

F32 = jnp.float32
BF16 = jnp.bfloat16


def _params(*semantics):
    return pltpu.CompilerParams(dimension_semantics=semantics,
                                vmem_limit_bytes=VMEM_LIMIT_BYTES)


def _dot(a, b):
    return jnp.dot(a, b, preferred_element_type=F32)


def _dot_nt(a, b):
    return lax.dot_general(a, b, (((1,), (1,)), ((), ())), preferred_element_type=F32)


def _sigmoid(x):
    return 0.5 * (1.0 + jnp.tanh(0.5 * x))


def _mod_kernel(c_ref, w_ref, b_ref, o_ref):
    c = c_ref[...]
    s = (c * _sigmoid(c)).astype(BF16)
    o_ref[0] = _dot(s, w_ref[0].astype(BF16)) + b_ref[0]


def _modulation(cond, w_mod, b_mod, *, tn=1024):
    depth, d, n = w_mod.shape
    return pl.pallas_call(
        _mod_kernel,
        grid=(depth, n // tn),
        in_specs=[
            pl.BlockSpec((MOD_ROWS, d), lambda l, j: (0, 0)),
            pl.BlockSpec((1, d, tn), lambda l, j: (l, 0, j)),
            pl.BlockSpec((1, 1, tn), lambda l, j: (l, 0, j)),
        ],
        out_specs=pl.BlockSpec((1, MOD_ROWS, tn), lambda l, j: (l, 0, j)),
        out_shape=jax.ShapeDtypeStruct((depth, MOD_ROWS, n), F32),
        compiler_params=_params("arbitrary", "arbitrary"),
        name="modulation",
    )(cond, w_mod, b_mod.reshape(depth, 1, n))


def _norm_mm_kernel(x_ref, mod_ref, g_ref, *rest, n_w, shift_idx, scale_idx, n_chunks):
    w_refs, o_ref, u_ref, wb_refs = rest[:n_w], rest[n_w], rest[n_w + 1], rest[n_w + 2:]
    s = pl.program_id(0)
    cm = x_ref.shape[0]
    slot = jnp.maximum(s - n_chunks, 0) % 2

    def cast_weights(into):
        for w_ref, wb_ref in zip(w_refs, wb_refs):
            wb_ref[into] = w_ref[0].astype(BF16)

    def matmul_rows(row0, rows, use):
        sl = pl.ds(pl.multiple_of(row0, rows), rows)
        u = u_ref[sl, :]
        a = _dot(u, wb_refs[0][use])
        if n_w == 2:
            a = a * _sigmoid(a) * _dot(u, wb_refs[1][use])
        o_ref[sl, :] = a.astype(o_ref.dtype)

    def norm_chunk():
        gain = g_ref[0] * (1.0 + mod_ref[0, scale_idx:scale_idx + 1, :])
        shift = mod_ref[0, shift_idx:shift_idx + 1, :]
        for p in range(cm // LANES):
            x = x_ref[p * LANES:(p + 1) * LANES, :]
            y = x * lax.rsqrt(jnp.mean(x * x, axis=-1, keepdims=True) + EPS)
            u_ref[pl.ds(pl.multiple_of(s * cm + p * LANES, LANES), LANES), :] = (y * gain + shift).astype(BF16)

    @pl.when(s == 0)
    def _():
        cast_weights(0)
        norm_chunk()

    @pl.when((s >= 1) & (s < n_chunks))
    def _():
        matmul_rows((s - 1) * cm, cm, 0)
        norm_chunk()

    @pl.when(s == n_chunks)
    def _():
        matmul_rows((n_chunks - 1) * cm, cm, 0)
        cast_weights(1)

    @pl.when(s > n_chunks)
    def _():
        for r in range(u_ref.shape[0] // MM_ROWS):
            matmul_rows(r * MM_ROWS, MM_ROWS, slot)
        cast_weights(1 - slot)


def _norm_matmul(x, mod, gain, weights, layer, *, shift_idx, scale_idx, rows_per_mod, tn, name,
                 cm=512):
    m, d = x.shape
    n = weights[0].shape[2]
    n_chunks = m // cm
    n_tiles = n // tn
    chunks_per_mod = rows_per_mod // cm
    chunk = lambda s: jnp.minimum(s, n_chunks - 1)
    tile = lambda s: jnp.maximum(s - n_chunks, 0)
    next_tile = lambda s: jnp.minimum(tile(s + 1), n_tiles - 1)
    kernel = functools.partial(_norm_mm_kernel, n_w=len(weights), shift_idx=shift_idx,
                               scale_idx=scale_idx, n_chunks=n_chunks)
    return pl.pallas_call(
        kernel,
        grid=(n_chunks + n_tiles,),
        in_specs=[
            pl.BlockSpec((cm, d), lambda s: (chunk(s), 0)),
            pl.BlockSpec((1, 6, d), lambda s: (chunk(s) // chunks_per_mod, 0, 0)),
            pl.BlockSpec((1, 1, d), lambda s: (layer, 0, 0)),
        ] + [pl.BlockSpec((1, d, tn), lambda s: (layer, 0, next_tile(s))) for _ in weights],
        out_specs=pl.BlockSpec((m, tn), lambda s: (0, tile(s))),
        out_shape=jax.ShapeDtypeStruct((m, n), BF16),
        scratch_shapes=[pltpu.VMEM((m, d), BF16)] + [pltpu.VMEM((2, d, tn), BF16) for _ in weights],
        compiler_params=_params("arbitrary"),
        name=name,
    )(x, mod, gain.reshape(gain.shape[0], 1, d), *weights)


def _mm_res_kernel(*refs, splits, gate_idx, rows_per_mod, n_chunks):
    n = len(splits)
    a_refs = refs[:n]
    w_ref, h_ref, mod_ref, o_ref, a_res, wb_ref = refs[n:]
    s = pl.program_id(1)
    tm = h_ref.shape[0]
    ca = a_refs[0].shape[0]
    tile_row0 = pl.program_id(0) * tm
    slot = jnp.maximum(s - n_chunks + 1, 0) % 2

    def cast_weights(into):
        wb_ref[into] = w_ref[0].astype(BF16)

    def product(row0, rows, use):
        sl = pl.ds(pl.multiple_of(row0, rows), rows)
        gate = mod_ref[(tile_row0 + row0) // rows_per_mod, gate_idx:gate_idx + 1, :]
        o_ref[sl, :] = h_ref[sl, :] + gate * _dot(a_res[sl, :], wb_ref[use])

    def stage_chunk():
        rows = pl.ds(pl.multiple_of(s * ca, ca), ca)
        k0 = 0
        for a_ref, kk in zip(a_refs, splits):
            a_res[rows, k0:k0 + kk] = a_ref[...]
            k0 += kk
        product(s * ca, ca, 0)

    @pl.when(s == 0)
    def _():
        cast_weights(0)
        stage_chunk()

    @pl.when((s >= 1) & (s < n_chunks - 1))
    def _():
        stage_chunk()

    @pl.when(s == n_chunks - 1)
    def _():
        stage_chunk()
        cast_weights(1)

    @pl.when(s >= n_chunks)
    def _():
        for r in range(tm // MM_ROWS):
            product(r * MM_ROWS, MM_ROWS, slot)
        cast_weights(1 - slot)


def _matmul_residual(a_parts, w, layer, h, mod, *, gate_idx, rows_per_mod, tm, tn, ca, name):
    m, n = h.shape
    k = w.shape[1]
    splits = tuple(a.shape[1] for a in a_parts)
    n_chunks = tm // ca
    n_tiles = n // tn
    assert n_chunks >= 2
    chunk = lambda i, s: i * n_chunks + jnp.minimum(s, n_chunks - 1)
    tile = lambda s: jnp.maximum(s - n_chunks + 1, 0)
    next_tile = lambda s: jnp.minimum(tile(s + 1), n_tiles - 1)
    kernel = functools.partial(_mm_res_kernel, splits=splits, gate_idx=gate_idx,
                               rows_per_mod=rows_per_mod, n_chunks=n_chunks)
    return pl.pallas_call(
        kernel,
        grid=(m // tm, n_chunks + n_tiles - 1),
        in_specs=[pl.BlockSpec((ca, kk), lambda i, s: (chunk(i, s), 0)) for kk in splits] + [
            pl.BlockSpec((1, k, tn), lambda i, s: (layer, 0, next_tile(s))),
            pl.BlockSpec((tm, tn), lambda i, s: (i, tile(s))),
            pl.BlockSpec((mod.shape[0], 6, tn), lambda i, s: (0, 0, tile(s))),
        ],
        out_specs=pl.BlockSpec((tm, tn), lambda i, s: (i, tile(s))),
        out_shape=jax.ShapeDtypeStruct((m, n), F32),
        scratch_shapes=[pltpu.VMEM((tm, k), BF16), pltpu.VMEM((2, k, tn), BF16)],
        compiler_params=_params("arbitrary", "arbitrary"),
        name=name,
    )(*a_parts, w, h, mod)


def _head_rms(x, gain):
    return x * lax.rsqrt(jnp.mean(x * x, axis=-1, keepdims=True) + EPS) * gain


def _ctx_attn_kernel(q_ref, k_ref, v_ref, qg_ref, kg_ref, kacc_ref, vacc_ref,
                     att_ref, kout_ref, vout_ref):
    del kacc_ref, vacc_ref
    qg = qg_ref[0]
    kg = kg_ref[0]
    for h in range(N_HEADS):
        sl = slice(h * HEAD_DIM, (h + 1) * HEAD_DIM)
        qn = _head_rms(q_ref[:, sl].astype(F32), qg)
        kn = _head_rms(k_ref[:, sl].astype(F32), kg)
        v = v_ref[:, sl]
        kout_ref[0, 0, :, sl] = kn
        vout_ref[0, 0, :, sl] = v.astype(F32)
        s = _dot_nt(qn.astype(BF16), kn.astype(BF16)) * LOGIT_SCALE
        e = jnp.exp2(s - jnp.max(s, axis=-1, keepdims=True))
        den = jnp.sum(e, axis=-1, keepdims=True)
        att_ref[:, sl] = (_dot(e.astype(BF16), v) / den).astype(BF16)


def _context_attention(proj, q_gain, k_gain, layer, k_acc, v_acc, *, seq):
    m = proj.shape[0]
    blk = lambda c: pl.BlockSpec((seq, D_ATT), lambda b, c=c: (b, c))
    gain = pl.BlockSpec((1, 1, HEAD_DIM), lambda b: (layer, 0, 0))
    cache = pl.BlockSpec((1, 1, seq, D_ATT), lambda b: (b, layer, 0, 0))
    hbm = pl.BlockSpec(memory_space=pl.ANY)
    gains = lambda g: g.reshape(g.shape[0], 1, HEAD_DIM)
    return pl.pallas_call(
        _ctx_attn_kernel,
        grid=(m // seq,),
        in_specs=[blk(0), blk(1), blk(2), gain, gain, hbm, hbm],
        out_specs=[pl.BlockSpec((seq, D_ATT), lambda b: (b, 0)), cache, cache],
        out_shape=[jax.ShapeDtypeStruct((m, D_ATT), BF16),
                   jax.ShapeDtypeStruct(k_acc.shape, F32),
                   jax.ShapeDtypeStruct(v_acc.shape, F32)],
        input_output_aliases={5: 1, 6: 2},
        compiler_params=_params("arbitrary"),
        name="context_attention",
    )(proj, proj, proj, gains(q_gain), gains(k_gain), k_acc, v_acc)


def _qk_norm_kernel(q_ref, k_ref, qg_ref, kg_ref, qo_ref, ko_ref):
    qg = qg_ref[0]
    kg = kg_ref[0]
    for h in range(N_HEADS):
        sl = slice(h * HEAD_DIM, (h + 1) * HEAD_DIM)
        qo_ref[:, sl] = _head_rms(q_ref[:, sl].astype(F32), qg).astype(BF16)
        ko_ref[:, sl] = _head_rms(k_ref[:, sl].astype(F32), kg).astype(BF16)


def _qk_norm(proj, q_gain, k_gain, layer, *, tm=512):
    m = proj.shape[0]
    blk = lambda c: pl.BlockSpec((tm, D_ATT), lambda i, c=c: (i, c))
    gain = pl.BlockSpec((1, 1, HEAD_DIM), lambda i: (layer, 0, 0))
    out = pl.BlockSpec((tm, D_ATT), lambda i: (i, 0))
    gains = lambda g: g.reshape(g.shape[0], 1, HEAD_DIM)
    return pl.pallas_call(
        _qk_norm_kernel,
        grid=(m // tm,),
        in_specs=[blk(0), blk(1), gain, gain],
        out_specs=[out, out],
        out_shape=[jax.ShapeDtypeStruct((m, D_ATT), BF16)] * 2,
        compiler_params=_params("arbitrary"),
        name="qk_norm",
    )(proj, proj, gains(q_gain), gains(k_gain))


def _na_block_geometry():
    rows = 32
    return ((0, 0), (NA_QROWS, 0), (rows - NA_QROWS, rows - NA_KROWS)), rows


def _na_bias_kernel(rel_ref, o_ref):
    qc = lax.broadcasted_iota(jnp.int32, (GRID_W, GRID_W), 0)
    kc = lax.broadcasted_iota(jnp.int32, (GRID_W, GRID_W), 1)
    cs = jnp.clip(qc - NA_COLS // 2, 0, GRID_W - NA_COLS)
    col_ok = (kc >= cs) & (kc < cs + NA_COLS)
    dc = kc - qc + (NA_COLS - 1)
    hits = [dc == j for j in range(2 * NA_COLS - 1)]
    masked = jnp.full((GRID_W, GRID_W), MASK_VALUE, F32)
    tiles = []
    for dr in range(2 * NA_ROWS - 1):
        x = masked
        for j, hit in enumerate(hits):
            x = jnp.where(hit, rel_ref[0, 0, dr, j] * LOG2_E, x)
        tiles.append(jnp.where(col_ok, x, MASK_VALUE))
    kinds, rows = _na_block_geometry()
    for kind, (r0, kb) in enumerate(kinds):
        for qr in range(NA_QROWS):
            r = r0 + qr
            rs = min(max(r - NA_ROWS // 2, 0), rows - NA_ROWS)
            for kr in range(NA_KROWS):
                krow = kb + kr
                tile = tiles[krow - r + NA_ROWS - 1] if rs <= krow < rs + NA_ROWS else masked
                o_ref[0, kind, 0, qr * GRID_W:(qr + 1) * GRID_W,
                      kr * GRID_W:(kr + 1) * GRID_W] = tile


def _na_bias(na_bias):
    depth = na_bias.shape[0]
    return pl.pallas_call(
        _na_bias_kernel,
        grid=(depth, N_HEADS),
        in_specs=[pl.BlockSpec((1, 1) + na_bias.shape[2:], lambda l, h: (l, h, 0, 0),
                               memory_space=pltpu.SMEM)],
        out_specs=pl.BlockSpec((1, 3, 1, NA_QBLK, NA_KBLK), lambda l, h: (l, 0, h, 0, 0)),
        out_shape=jax.ShapeDtypeStruct((depth, 3, N_HEADS, NA_QBLK, NA_KBLK), F32),
        compiler_params=_params("arbitrary", "arbitrary"),
        name="na_bias",
    )(na_bias)


def _na_attn_kernel(q_ref, k_ref, v_ref, kc_ref, vc_ref, bias_ref, o_ref, *, n_blocks):
    blk = pl.program_id(1)
    start = pl.multiple_of(jnp.clip(blk - 1, 0, n_blocks - 3) * NA_QBLK, NA_QBLK)
    for h in range(N_HEADS):
        sl = slice(h * HEAD_DIM, (h + 1) * HEAD_DIM)
        q = q_ref[:, sl]
        kw = k_ref[pl.ds(start, NA_KBLK), sl]
        vw = v_ref[pl.ds(start, NA_KBLK), sl]
        s_loc = _dot_nt(q, kw) * LOGIT_SCALE + bias_ref[0, 0, h]
        s_ctx = _dot_nt(q, kc_ref[0, 0, :, sl].astype(BF16)) * LOGIT_SCALE
        mx = jnp.maximum(jnp.max(s_loc, axis=-1, keepdims=True),
                         jnp.max(s_ctx, axis=-1, keepdims=True))
        e_loc = jnp.exp2(s_loc - mx)
        e_ctx = jnp.exp2(s_ctx - mx)
        den = jnp.sum(e_loc, axis=-1, keepdims=True) + jnp.sum(e_ctx, axis=-1, keepdims=True)
        o = _dot(e_loc.astype(BF16), vw) + _dot(e_ctx.astype(BF16), vc_ref[0, 0, :, sl].astype(BF16))
        o_ref[:, sl] = (o / den).astype(BF16)


def _neighbourhood_attention(qn, kn, proj, k_ctx, v_ctx, bias, layer, *, seq):
    m = qn.shape[0]
    batch = m // seq
    n_blocks = seq // NA_QBLK
    past = k_ctx.shape[2]
    kind = lambda i: jnp.where(i == 0, 0, jnp.where(i == n_blocks - 1, 2, 1))
    ctx = pl.BlockSpec((1, 1, past, D_ATT), lambda b, i: (b, layer, 0, 0))
    return pl.pallas_call(
        functools.partial(_na_attn_kernel, n_blocks=n_blocks),
        grid=(batch, n_blocks),
        in_specs=[
            pl.BlockSpec((NA_QBLK, D_ATT), lambda b, i: (b * n_blocks + i, 0)),
            pl.BlockSpec((seq, D_ATT), lambda b, i: (b, 0)),
            pl.BlockSpec((seq, D_ATT), lambda b, i: (b, 2)),
            ctx, ctx,
            pl.BlockSpec((1, 1, N_HEADS, NA_QBLK, NA_KBLK), lambda b, i: (layer, kind(i), 0, 0, 0)),
        ],
        out_specs=pl.BlockSpec((NA_QBLK, D_ATT), lambda b, i: (b * n_blocks + i, 0)),
        out_shape=jax.ShapeDtypeStruct((m, D_ATT), BF16),
        compiler_params=_params("arbitrary", "arbitrary"),
        name="neighbourhood_attention",
    )(qn, kn, proj, k_ctx, v_ctx, bias)


def _stage_padded(pad_ref, x):
    zeros = jnp.zeros((PAD, pad_ref.shape[1]), F32)
    rows = x.shape[0]
    pad_ref[0:PAD, :] = zeros
    pad_ref[PAD + rows:2 * PAD + rows, :] = zeros
    pad_ref[PAD:PAD + rows, :] = x


def _conv_taps(pad_ref, seg, w, left, isolated):
    base = seg * SEG_LEN
    t = lax.broadcasted_iota(jnp.int32, (SEG_LEN, 1), 0)
    acc = None
    for k in range(w.shape[0]):
        off = k - left
        xs = pad_ref[PAD + base + off:PAD + base + off + SEG_LEN, :]
        if isolated and off != 0:
            xs = jnp.where((t + off >= 0) & (t + off < SEG_LEN), xs, 0.0)
        term = xs * w[k:k + 1, :]
        acc = term if acc is None else acc + term
    return acc


def _gelu_tanh(x):
    return x * (0.5 * (1.0 + jnp.tanh(0.7978845608028654 * (x + 0.044715 * (x * x * x)))))


def _lru_kernel(lxy_ref, h0_ref, cw_ref, cb_ref, w_ref, b_ref, lam_ref, *rest, chained):
    if chained:
        y_ref, xpad, a_tm, u_tm, p_tm = rest
    else:
        y_ref, fin_ref, xpad, a_tm, u_tm = rest
    last = (SEG_LEN - 1) * SEG_PER_BLOCK
    _stage_padded(xpad, lxy_ref[:, 0:D_LRU].astype(F32))
    cw = cw_ref[0]
    neg_lam = -lam_ref[0]
    softplus = jnp.maximum(neg_lam, 0.0) + jnp.log1p(jnp.exp(-jnp.abs(neg_lam)))
    decay = (-0.5 * LRU_C) * softplus
    for c in range(SEG_PER_BLOCK):
        xc = cb_ref[0] + _conv_taps(xpad, c, cw, LRU_CONV // 2, not chained)
        seg_rows = pl.ds(c, SEG_LEN, stride=SEG_PER_BLOCK)
        for n in range(LRU_BLOCKS):
            cols = slice(n * LRU_BW, (n + 1) * LRU_BW)
            xb = xc[:, cols]
            g = _dot(xb.astype(BF16), w_ref[0, n].astype(BF16)) + b_ref[0, n:n + 1, :]
            x_half = 0.5 * xb
            for d in range(2):
                ta = jnp.tanh(g[:, (2 * d) * LRU_BW:(2 * d + 1) * LRU_BW])
                ti = jnp.tanh(g[:, (2 * d + 1) * LRU_BW:(2 * d + 2) * LRU_BW])
                log_a = decay[d:d + 1, cols] * (1.0 + ta)
                th = jnp.tanh(log_a)
                a_tm[d, n, seg_rows, :] = jnp.exp(log_a)
                u_tm[d, n, seg_rows, :] = (jnp.sqrt(-2.0 * th) * lax.rsqrt(1.0 - th)) * ((1.0 + ti) * x_half)

    def scan_dir(d, rows, h, p):
        h_new, p_new = [], []
        for n in range(LRU_BLOCKS):
            a = a_tm[d, n, rows, :]
            hn = a * h[n] + u_tm[d, n, rows, :]
            u_tm[d, n, rows, :] = hn
            h_new.append(hn)
            if chained:
                pn = a * p[n]
                p_tm[d, n, rows, :] = pn
                p_new.append(pn)
        return tuple(h_new), tuple(p_new)

    def step(t, carry):
        hf, pf, hb, pb = carry
        hf, pf = scan_dir(0, pl.ds(pl.multiple_of(t * SEG_PER_BLOCK, SEG_PER_BLOCK), SEG_PER_BLOCK), hf, pf)
        tb = SEG_LEN - 1 - t
        hb, pb = scan_dir(1, pl.ds(pl.multiple_of(tb * SEG_PER_BLOCK, SEG_PER_BLOCK), SEG_PER_BLOCK), hb, pb)
        return hf, pf, hb, pb

    lane_blocks = [slice(n * LRU_BW, (n + 1) * LRU_BW) for n in range(LRU_BLOCKS)]
    if chained:
        zero = tuple(jnp.zeros((SEG_PER_BLOCK, LRU_BW), F32) for _ in lane_blocks)
        one = tuple(jnp.ones((SEG_PER_BLOCK, LRU_BW), F32) for _ in lane_blocks)
        init = (zero, one, zero, one)
    else:
        init = (tuple(h0_ref[0, :, cols] for cols in lane_blocks), (),
                tuple(h0_ref[1, :, cols] for cols in lane_blocks), ())
    lax.fori_loop(0, SEG_LEN, step, init, unroll=4)

    carry_f = carry_b = None
    if chained:
        carry_f, carry_b = [], []
        for n, cols in enumerate(lane_blocks):
            end_h, end_p = u_tm[0, n, last:last + SEG_PER_BLOCK, :], p_tm[0, n, last:last + SEG_PER_BLOCK, :]
            cf = [h0_ref[0, 0:1, cols]]
            for s in range(SEG_PER_BLOCK - 1):
                cf.append(end_h[s:s + 1] + end_p[s:s + 1] * cf[s])
            carry_f.append(cf)
            beg_h, beg_p = u_tm[1, n, 0:SEG_PER_BLOCK, :], p_tm[1, n, 0:SEG_PER_BLOCK, :]
            cb = [None] * SEG_PER_BLOCK
            cb[SEG_PER_BLOCK - 1] = h0_ref[0, 1:2, cols]
            for s in range(SEG_PER_BLOCK - 1, 0, -1):
                cb[s - 1] = beg_h[s:s + 1] + beg_p[s:s + 1] * cb[s]
            carry_b.append(cb)
    else:
        for n, cols in enumerate(lane_blocks):
            fin_ref[0, :, cols] = u_tm[0, n, last:last + SEG_PER_BLOCK, :]
            fin_ref[1, :, cols] = u_tm[1, n, 0:SEG_PER_BLOCK, :]

    for c in range(SEG_PER_BLOCK):
        seg_rows = pl.ds(c, SEG_LEN, stride=SEG_PER_BLOCK)
        tok = slice(c * SEG_LEN, (c + 1) * SEG_LEN)
        for n, cols in enumerate(lane_blocks):
            h = u_tm[0, n, seg_rows, :] + u_tm[1, n, seg_rows, :]
            if chained:
                h = h + p_tm[0, n, seg_rows, :] * carry_f[n][c] + p_tm[1, n, seg_rows, :] * carry_b[n][c]
            ly = lxy_ref[tok, D_LRU + n * LRU_BW:D_LRU + (n + 1) * LRU_BW].astype(F32)
            y_ref[tok, cols] = (h * _gelu_tanh(ly)).astype(BF16)


def _rglru(proj, h0, conv_w, conv_b, w_cat, b_cat, lam, layer, *, chained):
    m = proj.shape[0]
    blocks = m // MIX_ROWS
    lxy_col = 3 * D_ATT // (2 * D_LRU)
    per_layer = lambda *shape: pl.BlockSpec((1,) + shape, lambda b: (layer,) + (0,) * len(shape))
    tm_buf = pltpu.VMEM((2, LRU_BLOCKS, MIX_ROWS, LRU_BW), F32)
    if chained:
        h0_spec = pl.BlockSpec((1, 2, D_LRU), lambda b: (b, 0, 0))
        out_specs = [pl.BlockSpec((MIX_ROWS, D_LRU), lambda b: (b, 0))]
        out_shape = [jax.ShapeDtypeStruct((m, D_LRU), BF16)]
        scratch = [tm_buf, tm_buf, tm_buf]
    else:
        h0_spec = pl.BlockSpec((2, SEG_PER_BLOCK, D_LRU), lambda b: (0, b, 0))
        out_specs = [pl.BlockSpec((MIX_ROWS, D_LRU), lambda b: (b, 0)),
                     pl.BlockSpec((2, SEG_PER_BLOCK, D_LRU), lambda b: (0, b, 0))]
        out_shape = [jax.ShapeDtypeStruct((m, D_LRU), BF16),
                     jax.ShapeDtypeStruct((2, blocks * SEG_PER_BLOCK, D_LRU), F32)]
        scratch = [tm_buf, tm_buf]
    return pl.pallas_call(
        functools.partial(_lru_kernel, chained=chained),
        grid=(blocks,),
        in_specs=[
            pl.BlockSpec((MIX_ROWS, 2 * D_LRU), lambda b: (b, lxy_col)),
            h0_spec,
            per_layer(LRU_CONV, D_LRU), per_layer(1, D_LRU),
            per_layer(LRU_BLOCKS, LRU_BW, 4 * LRU_BW), per_layer(LRU_BLOCKS, 4 * LRU_BW),
            per_layer(2, D_LRU),
        ],
        out_specs=out_specs,
        out_shape=out_shape,
        scratch_shapes=[pltpu.VMEM((MIX_ROWS + 2 * PAD, D_LRU), F32)] + scratch,
        compiler_params=_params("arbitrary"),
        name="rglru",
    )(proj, h0, conv_w, conv_b.reshape(conv_b.shape[0], 1, D_LRU), w_cat, b_cat, lam)


def _sconv_kernel(b_ref, c_ref, x_ref, w_ref, y_ref, pad, *, isolated):
    _stage_padded(pad, c_ref[...].astype(F32) * x_ref[...].astype(F32))
    w = w_ref[0]
    for c in range(SEG_PER_BLOCK):
        tok = slice(c * SEG_LEN, (c + 1) * SEG_LEN)
        conv = _conv_taps(pad, c, w, SC_CONV // 2, isolated)
        y_ref[tok, :] = (b_ref[tok, :].astype(F32) * conv).astype(BF16)


def _short_conv(proj, w, layer, *, isolated):
    m = proj.shape[0]
    col0 = (3 * D_ATT + 2 * D_LRU) // D_SC
    blk = lambda part: pl.BlockSpec((MIX_ROWS, D_SC), lambda b, part=part: (b, col0 + part))
    return pl.pallas_call(
        functools.partial(_sconv_kernel, isolated=isolated),
        grid=(m // MIX_ROWS,),
        in_specs=[blk(0), blk(1), blk(2), pl.BlockSpec((1, SC_CONV, D_SC), lambda b: (layer, 0, 0))],
        out_specs=pl.BlockSpec((MIX_ROWS, D_SC), lambda b: (b, 0)),
        out_shape=jax.ShapeDtypeStruct((m, D_SC), BF16),
        scratch_shapes=[pltpu.VMEM((MIX_ROWS + 2 * PAD, D_SC), F32)],
        compiler_params=_params("arbitrary"),
        name="short_conv",
    )(proj, proj, proj, w)


def _lru_gate_weights(w_a, b_a, w_i, b_i):
    depth = w_a.shape[0]
    w_cat = jnp.concatenate([w_a[:, 0], w_i[:, 0], w_a[:, 1], w_i[:, 1]], axis=-1)
    blocks = lambda b: b.reshape(depth, 2, LRU_BLOCKS, LRU_BW)
    ba, bi = blocks(b_a), blocks(b_i)
    b_cat = jnp.concatenate([ba[:, 0], bi[:, 0], ba[:, 1], bi[:, 1]], axis=-1)
    return 0.5 * w_cat, 0.5 * b_cat


def _stream_layer(h, mod, p, layer, *, rows_per_mod, ctx, caches):
    m = h.shape[0]
    proj = _norm_matmul(h, mod, p["g_mix"], [p["w_in"]], layer, shift_idx=0, scale_idx=1,
                        rows_per_mod=rows_per_mod, tn=512, name="in_proj")
    if ctx is None:
        k_acc, v_acc = caches
        att, k_acc, v_acc = _context_attention(proj, p["q_gain"], p["k_gain"], layer, k_acc, v_acc,
                                               seq=SEG_LEN)
        caches = (k_acc, v_acc)
        h0 = jnp.zeros((2, m // SEG_LEN, D_LRU), F32)
        lru, lru_final = _rglru(proj, h0, p["lru_conv_w"], p["lru_conv_b"], p["w_cat"], p["b_cat"],
                                p["lru_lambda"], layer, chained=False)
    else:
        k_ctx, v_ctx, h0, bias = ctx
        qn, kn = _qk_norm(proj, p["q_gain"], p["k_gain"], layer)
        att = _neighbourhood_attention(qn, kn, proj, k_ctx, v_ctx, bias, layer, seq=MIX_ROWS)
        (lru,) = _rglru(proj, h0, p["lru_conv_w"], p["lru_conv_b"], p["w_cat"], p["b_cat"],
                        p["lru_lambda"], layer, chained=True)
        lru_final = None
    conv = _short_conv(proj, p["sc_conv_w"], layer, isolated=ctx is None)
    h = _matmul_residual([att, lru, conv], p["w_out"], layer, h, mod, gate_idx=2,
                         rows_per_mod=rows_per_mod, tm=m, tn=256, ca=512, name="out_proj")
    hid = _norm_matmul(h, mod, p["g_ffn"], [p["w_ffn_gate"], p["w_ffn_up"]], layer, shift_idx=3,
                       scale_idx=4, rows_per_mod=rows_per_mod, tn=256, name="ffn_up")
    h = _matmul_residual([hid], p["w_ffn_down"], layer, h, mod, gate_idx=5,
                         rows_per_mod=rows_per_mod, tm=m // 2, tn=256, ca=256, name="ffn_down")
    return h, caches, lru_final


def kernel(x_prompt, x_sample, cache_k, cache_v, state_lru, c, c_ctx, w_mod, b_mod, g_mix, g_ffn, w_in, q_gain, k_gain, na_bias, lru_conv_w, lru_conv_b, lru_w_a, lru_b_a, lru_w_i, lru_b_i, lru_lambda, sc_conv_w, w_out, w_ffn_gate, w_ffn_up, w_ffn_down):
    batch, seq, d = x_prompt.shape
    dec_batch, dec_seq, _ = x_sample.shape
    depth = w_mod.shape[0]
    past = cache_k.shape[2]
    assert seq == SEG_LEN and dec_seq == MIX_ROWS and batch % SEG_PER_BLOCK == 0

    cond = jnp.zeros((MOD_ROWS, d), F32).at[0].set(c_ctx).at[1:1 + dec_batch].set(c)
    mod_all = _modulation(cond, w_mod, b_mod).reshape(depth, MOD_ROWS, 6, d)
    bias_all = _na_bias(na_bias)
    w_cat, b_cat = _lru_gate_weights(lru_w_a, lru_b_a, lru_w_i, lru_b_i)
    p = {
        "g_mix": g_mix, "g_ffn": g_ffn, "w_in": w_in, "q_gain": q_gain, "k_gain": k_gain,
        "lru_conv_w": lru_conv_w, "lru_conv_b": lru_conv_b, "w_cat": w_cat, "b_cat": b_cat,
        "lru_lambda": lru_lambda, "sc_conv_w": sc_conv_w, "w_out": w_out,
        "w_ffn_gate": w_ffn_gate, "w_ffn_up": w_ffn_up, "w_ffn_down": w_ffn_down,
    }
    k_ctx = cache_k.reshape(dec_batch, depth, past, D_ATT)
    v_ctx = cache_v.reshape(dec_batch, depth, past, D_ATT)

    hp = x_prompt.reshape(batch * seq, d)
    hs = x_sample.reshape(dec_batch * dec_seq, d)
    caches = (jnp.zeros((batch, depth, seq, D_ATT), F32), jnp.zeros((batch, depth, seq, D_ATT), F32))
    new_s = []
    for l in range(depth):
        hp, caches, s_l = _stream_layer(hp, mod_all[l, 0:1], p, l, rows_per_mod=batch * seq,
                                        ctx=None, caches=caches)
        new_s.append(jnp.swapaxes(s_l, 0, 1))
        ctx = (k_ctx, v_ctx, state_lru[:, l], bias_all)
        hs, _, _ = _stream_layer(hs, mod_all[l, 1:1 + dec_batch], p, l, rows_per_mod=dec_seq,
                                 ctx=ctx, caches=None)
    cache_shape = (batch, depth, seq, N_HEADS, HEAD_DIM)
    return (hp.reshape(batch, seq, d), hs.reshape(dec_batch, dec_seq, d),
            caches[0].reshape(cache_shape), caches[1].reshape(cache_shape), jnp.stack(new_s, axis=1))
```

```python
import functools

import jax
import jax.numpy as jnp
from jax import lax
from jax.experimental import pallas as pl
from jax.experimental.pallas import tpu as pltpu

D_MODEL = 2048
N_HEADS = 8
HEAD_DIM = 128
D_ATT = N_HEADS * HEAD_DIM
D_LRU = 512
LRU_BLOCKS = 4
LRU_BW = 128
LRU_CONV = 4
LRU_C = 8.0
D_SC = 512
SC_CONV = 3
GRID_W = 64
NA_ROWS = 8
NA_COLS = 16
EPS = 1e-6
LOG2_E = 1.4426950408889634
LOGIT_SCALE = HEAD_DIM ** -0.5 * LOG2_E

NA_QROWS = 4
NA_KROWS = 12
NA_QBLK = NA_QROWS * GRID_W
NA_KBLK = NA_KROWS * GRID_W
MASK_VALUE = -1e30

LANES = 128
SUBLANES = 8
MOD_ROWS = SUBLANES
VMEM_LIMIT_BYTES = 60 * 1024 * 1024

SEG_LEN = 256
SEG_PER_BLOCK = SUBLANES
MIX_ROWS = SEG_LEN * SEG_PER_BLOCK
PAD = SUBLANES
MM_ROWS = 1024

F32 = jnp.float32
BF16 = jnp.bfloat16
F32_MIN_NORMAL = 1.1754943508222875e-38


def _params(*semantics):
    return pltpu.CompilerParams(dimension_semantics=semantics,
                                vmem_limit_bytes=VMEM_LIMIT_BYTES)


def _dot(a, b):
    return jnp.dot(a, b, preferred_element_type=F32)


def _dot_nt(a, b):
    return lax.dot_general(a, b, (((1,), (1,)), ((), ())), preferred_element_type=F32)


def _sigmoid(x):
    return 0.5 * (1.0 + jnp.tanh(0.5 * x))


def _mod_kernel(c_ref, w_ref, b_ref, o_ref):
    c = c_ref[...]
    s = (c * _sigmoid(c)).astype(BF16)
    o_ref[0] = _dot(s, w_ref[0].astype(BF16)) + b_ref[0]


def _modulation(cond, w_mod, b_mod, *, tn=1024):
    depth, d, n = w_mod.shape
    return pl.pallas_call(
        _mod_kernel,
        grid=(depth, n // tn),
        in_specs=[
            pl.BlockSpec((MOD_ROWS, d), lambda l, j: (0, 0)),
            pl.BlockSpec((1, d, tn), lambda l, j: (l, 0, j)),
            pl.BlockSpec((1, 1, tn), lambda l, j: (l, 0, j)),
        ],
        out_specs=pl.BlockSpec((1, MOD_ROWS, tn), lambda l, j: (l, 0, j)),
        out_shape=jax.ShapeDtypeStruct((depth, MOD_ROWS, n), F32),
        compiler_params=_params("arbitrary", "arbitrary"),
        name="modulation",
    )(cond, w_mod, b_mod.reshape(depth, 1, n))


def _norm_mm_kernel(x_ref, mod_ref, g_ref, *rest, n_w, shift_idx, scale_idx, n_chunks):
    w_refs, o_ref, u_ref, wb_refs = rest[:n_w], rest[n_w], rest[n_w + 1], rest[n_w + 2:]
    s = pl.program_id(0)
    cm = x_ref.shape[0]
    slot = jnp.maximum(s - n_chunks, 0) % 2

    def cast_weights(into):
        for w_ref, wb_ref in zip(w_refs, wb_refs):
            wb_ref[into] = w_ref[0].astype(BF16)

    def matmul_rows(row0, rows, use):
        sl = pl.ds(pl.multiple_of(row0, rows), rows)
        u = u_ref[sl, :]
        a = _dot(u, wb_refs[0][use])
        if n_w == 2:
            a = a * _sigmoid(a) * _dot(u, wb_refs[1][use])
        o_ref[sl, :] = a.astype(o_ref.dtype)

    def norm_chunk():
        gain = g_ref[0] * (1.0 + mod_ref[0, scale_idx:scale_idx + 1, :])
        shift = mod_ref[0, shift_idx:shift_idx + 1, :]
        for p in range(cm // LANES):
            x = x_ref[p * LANES:(p + 1) * LANES, :]
            y = x * lax.rsqrt(jnp.mean(x * x, axis=-1, keepdims=True) + EPS)
            u_ref[pl.ds(pl.multiple_of(s * cm + p * LANES, LANES), LANES), :] = (y * gain + shift).astype(BF16)

    @pl.when(s == 0)
    def _():
        cast_weights(0)
        norm_chunk()

    @pl.when((s >= 1) & (s < n_chunks))
    def _():
        matmul_rows((s - 1) * cm, cm, 0)
        norm_chunk()

    @pl.when(s == n_chunks)
    def _():
        matmul_rows((n_chunks - 1) * cm, cm, 0)
        cast_weights(1)

    @pl.when(s > n_chunks)
    def _():
        for r in range(u_ref.shape[0] // MM_ROWS):
            matmul_rows(r * MM_ROWS, MM_ROWS, slot)
        cast_weights(1 - slot)


def _norm_matmul(x, mod, gain, weights, layer, *, shift_idx, scale_idx, rows_per_mod, tn, name,
                 cm=512):
    m, d = x.shape
    n = weights[0].shape[2]
    n_chunks = m // cm
    n_tiles = n // tn
    chunks_per_mod = rows_per_mod // cm
    chunk = lambda s: jnp.minimum(s, n_chunks - 1)
    tile = lambda s: jnp.maximum(s - n_chunks, 0)
    next_tile = lambda s: jnp.minimum(tile(s + 1), n_tiles - 1)
    kernel = functools.partial(_norm_mm_kernel, n_w=len(weights), shift_idx=shift_idx,
                               scale_idx=scale_idx, n_chunks=n_chunks)
    return pl.pallas_call(
        kernel,
        grid=(n_chunks + n_tiles,),
        in_specs=[
            pl.BlockSpec((cm, d), lambda s: (chunk(s), 0)),
            pl.BlockSpec((1, 6, d), lambda s: (chunk(s) // chunks_per_mod, 0, 0)),
            pl.BlockSpec((1, 1, d), lambda s: (layer, 0, 0)),
        ] + [pl.BlockSpec((1, d, tn), lambda s: (layer, 0, next_tile(s))) for _ in weights],
        out_specs=pl.BlockSpec((m, tn), lambda s: (0, tile(s))),
        out_shape=jax.ShapeDtypeStruct((m, n), BF16),
        scratch_shapes=[pltpu.VMEM((m, d), BF16)] + [pltpu.VMEM((2, d, tn), BF16) for _ in weights],
        compiler_params=_params("arbitrary"),
        name=name,
    )(x, mod, gain.reshape(gain.shape[0], 1, d), *weights)


def _mm_res_kernel(*refs, splits, gate_idx, rows_per_mod, n_chunks):
    n = len(splits)
    a_refs = refs[:n]
    w_ref, h_ref, mod_ref, o_ref, a_res, wb_ref = refs[n:]
    s = pl.program_id(1)
    tm = h_ref.shape[0]
    ca = a_refs[0].shape[0]
    tile_row0 = pl.program_id(0) * tm
    slot = jnp.maximum(s - n_chunks + 1, 0) % 2

    def cast_weights(into):
        wb_ref[into] = w_ref[0].astype(BF16)

    def product(row0, rows, use):
        sl = pl.ds(pl.multiple_of(row0, rows), rows)
        gate = mod_ref[(tile_row0 + row0) // rows_per_mod, gate_idx:gate_idx + 1, :]
        o_ref[sl, :] = h_ref[sl, :] + gate * _dot(a_res[sl, :], wb_ref[use])

    def stage_chunk():
        rows = pl.ds(pl.multiple_of(s * ca, ca), ca)
        k0 = 0
        for a_ref, kk in zip(a_refs, splits):
            a_res[rows, k0:k0 + kk] = a_ref[...]
            k0 += kk
        product(s * ca, ca, 0)

    @pl.when(s == 0)
    def _():
        cast_weights(0)
        stage_chunk()

    @pl.when((s >= 1) & (s < n_chunks - 1))
    def _():
        stage_chunk()

    @pl.when(s == n_chunks - 1)
    def _():
        stage_chunk()
        cast_weights(1)

    @pl.when(s >= n_chunks)
    def _():
        for r in range(tm // MM_ROWS):
            product(r * MM_ROWS, MM_ROWS, slot)
        cast_weights(1 - slot)


def _matmul_residual(a_parts, w, layer, h, mod, *, gate_idx, rows_per_mod, tm, tn, ca, name):
    m, n = h.shape
    k = w.shape[1]
    splits = tuple(a.shape[1] for a in a_parts)
    n_chunks = tm // ca
    n_tiles = n // tn
    assert n_chunks >= 2
    chunk = lambda i, s: i * n_chunks + jnp.minimum(s, n_chunks - 1)
    tile = lambda s: jnp.maximum(s - n_chunks + 1, 0)
    next_tile = lambda s: jnp.minimum(tile(s + 1), n_tiles - 1)
    kernel = functools.partial(_mm_res_kernel, splits=splits, gate_idx=gate_idx,
                               rows_per_mod=rows_per_mod, n_chunks=n_chunks)
    return pl.pallas_call(
        kernel,
        grid=(m // tm, n_chunks + n_tiles - 1),
        in_specs=[pl.BlockSpec((ca, kk), lambda i, s: (chunk(i, s), 0)) for kk in splits] + [
            pl.BlockSpec((1, k, tn), lambda i, s: (layer, 0, next_tile(s))),
            pl.BlockSpec((tm, tn), lambda i, s: (i, tile(s))),
            pl.BlockSpec((mod.shape[0], 6, tn), lambda i, s: (0, 0, tile(s))),
        ],
        out_specs=pl.BlockSpec((tm, tn), lambda i, s: (i, tile(s))),
        out_shape=jax.ShapeDtypeStruct((m, n), F32),
        scratch_shapes=[pltpu.VMEM((tm, k), BF16), pltpu.VMEM((2, k, tn), BF16)],
        compiler_params=_params("arbitrary", "arbitrary"),
        name=name,
    )(*a_parts, w, h, mod)


def _head_rms(x, gain):
    return x * lax.rsqrt(jnp.mean(x * x, axis=-1, keepdims=True) + EPS) * gain


def _ctx_attn_kernel(qkv_ref, qg_ref, kg_ref, kacc_ref, vacc_ref, att_ref, kout_ref, vout_ref):
    del kacc_ref, vacc_ref
    qg = qg_ref[0]
    kg = kg_ref[0]
    for h in range(N_HEADS):
        sl = slice(h * HEAD_DIM, (h + 1) * HEAD_DIM)
        qn = _head_rms(qkv_ref[:, sl].astype(F32), qg)
        kn = _head_rms(qkv_ref[:, D_ATT + h * HEAD_DIM:D_ATT + (h + 1) * HEAD_DIM].astype(F32), kg)
        v = qkv_ref[:, 2 * D_ATT + h * HEAD_DIM:2 * D_ATT + (h + 1) * HEAD_DIM]
        kout_ref[0, 0, :, sl] = kn
        vout_ref[0, 0, :, sl] = v.astype(F32)
        s = _dot_nt(qn.astype(BF16), kn.astype(BF16)) * LOGIT_SCALE
        e = jnp.exp2(s - jnp.max(s, axis=-1, keepdims=True))
        den = jnp.sum(e, axis=-1, keepdims=True)
        att_ref[:, sl] = (_dot(e.astype(BF16), v) / den).astype(BF16)


def _context_attention(proj, q_gain, k_gain, layer, k_acc, v_acc, *, seq):
    m = proj.shape[0]
    gain = pl.BlockSpec((1, 1, HEAD_DIM), lambda b: (layer, 0, 0))
    cache = pl.BlockSpec((1, 1, seq, D_ATT), lambda b: (b, layer, 0, 0))
    hbm = pl.BlockSpec(memory_space=pl.ANY)
    gains = lambda g: g.reshape(g.shape[0], 1, HEAD_DIM)
    return pl.pallas_call(
        _ctx_attn_kernel,
        grid=(m // seq,),
        in_specs=[pl.BlockSpec((seq, 3 * D_ATT), lambda b: (b, 0)), gain, gain, hbm, hbm],
        out_specs=[pl.BlockSpec((seq, D_ATT), lambda b: (b, 0)), cache, cache],
        out_shape=[jax.ShapeDtypeStruct((m, D_ATT), BF16),
                   jax.ShapeDtypeStruct(k_acc.shape, F32),
                   jax.ShapeDtypeStruct(v_acc.shape, F32)],
        input_output_aliases={3: 1, 4: 2},
        compiler_params=_params("arbitrary"),
        name="context_attention",
    )(proj, gains(q_gain), gains(k_gain), k_acc, v_acc)


def _qk_norm_kernel(qk_ref, qg_ref, kg_ref, qo_ref, ko_ref):
    qg = qg_ref[0]
    kg = kg_ref[0]
    for h in range(N_HEADS):
        sl = slice(h * HEAD_DIM, (h + 1) * HEAD_DIM)
        qo_ref[:, sl] = _head_rms(qk_ref[:, sl].astype(F32), qg).astype(BF16)
        k = qk_ref[:, D_ATT + h * HEAD_DIM:D_ATT + (h + 1) * HEAD_DIM]
        ko_ref[:, sl] = _head_rms(k.astype(F32), kg).astype(BF16)


def _qk_norm(proj, q_gain, k_gain, layer, *, tm=512):
    m = proj.shape[0]
    gain = pl.BlockSpec((1, 1, HEAD_DIM), lambda i: (layer, 0, 0))
    out = pl.BlockSpec((tm, D_ATT), lambda i: (i, 0))
    gains = lambda g: g.reshape(g.shape[0], 1, HEAD_DIM)
    return pl.pallas_call(
        _qk_norm_kernel,
        grid=(m // tm,),
        in_specs=[pl.BlockSpec((tm, 2 * D_ATT), lambda i: (i, 0)), gain, gain],
        out_specs=[out, out],
        out_shape=[jax.ShapeDtypeStruct((m, D_ATT), BF16)] * 2,
        compiler_params=_params("arbitrary"),
        name="qk_norm",
    )(proj, gains(q_gain), gains(k_gain))


def _na_block_geometry():
    rows = 32
    return ((0, 0), (NA_QROWS, 0), (rows - NA_QROWS, rows - NA_KROWS)), rows


def _na_bias_kernel(rel_ref, o_ref):
    qc = lax.broadcasted_iota(jnp.int32, (GRID_W, GRID_W), 0)
    kc = lax.broadcasted_iota(jnp.int32, (GRID_W, GRID_W), 1)
    cs = jnp.clip(qc - NA_COLS // 2, 0, GRID_W - NA_COLS)
    col_ok = (kc >= cs) & (kc < cs + NA_COLS)
    dc = kc - qc + (NA_COLS - 1)
    hits = [dc == j for j in range(2 * NA_COLS - 1)]
    masked = jnp.full((GRID_W, GRID_W), MASK_VALUE, F32)
    tiles = []
    for dr in range(2 * NA_ROWS - 1):
        x = masked
        for j, hit in enumerate(hits):
            x = jnp.where(hit, rel_ref[0, 0, dr, j] * LOG2_E, x)
        tiles.append(jnp.where(col_ok, x, MASK_VALUE))
    kinds, rows = _na_block_geometry()
    for kind, (r0, kb) in enumerate(kinds):
        for qr in range(NA_QROWS):
            r = r0 + qr
            rs = min(max(r - NA_ROWS // 2, 0), rows - NA_ROWS)
            for kr in range(NA_KROWS):
                krow = kb + kr
                tile = tiles[krow - r + NA_ROWS - 1] if rs <= krow < rs + NA_ROWS else masked
                o_ref[0, kind, 0, qr * GRID_W:(qr + 1) * GRID_W,
                      kr * GRID_W:(kr + 1) * GRID_W] = tile


def _na_bias(na_bias):
    depth = na_bias.shape[0]
    return pl.pallas_call(
        _na_bias_kernel,
        grid=(depth, N_HEADS),
        in_specs=[pl.BlockSpec((1, 1) + na_bias.shape[2:], lambda l, h: (l, h, 0, 0),
                               memory_space=pltpu.SMEM)],
        out_specs=pl.BlockSpec((1, 3, 1, NA_QBLK, NA_KBLK), lambda l, h: (l, 0, h, 0, 0)),
        out_shape=jax.ShapeDtypeStruct((depth, 3, N_HEADS, NA_QBLK, NA_KBLK), F32),
        compiler_params=_params("arbitrary", "arbitrary"),
        name="na_bias",
    )(na_bias)


def _na_attn_kernel(q_ref, k_ref, v_ref, kc_ref, vc_ref, bias_ref, o_ref, *, n_blocks):
    blk = pl.program_id(1)
    start = pl.multiple_of(jnp.clip(blk - 1, 0, n_blocks - 3) * NA_QBLK, NA_QBLK)
    for h in range(N_HEADS):
        sl = slice(h * HEAD_DIM, (h + 1) * HEAD_DIM)
        q = q_ref[:, sl]
        kw = k_ref[pl.ds(start, NA_KBLK), sl]
        vw = v_ref[pl.ds(start, NA_KBLK), sl]
        s_loc = _dot_nt(q, kw) * LOGIT_SCALE + bias_ref[0, 0, h]
        s_ctx = _dot_nt(q, kc_ref[0, 0, :, sl].astype(BF16)) * LOGIT_SCALE
        mx = jnp.maximum(jnp.max(s_loc, axis=-1, keepdims=True),
                         jnp.max(s_ctx, axis=-1, keepdims=True))
        e_loc = jnp.exp2(s_loc - mx)
        e_ctx = jnp.exp2(s_ctx - mx)
        den = jnp.sum(e_loc, axis=-1, keepdims=True) + jnp.sum(e_ctx, axis=-1, keepdims=True)
        o = _dot(e_loc.astype(BF16), vw) + _dot(e_ctx.astype(BF16), vc_ref[0, 0, :, sl].astype(BF16))
        o_ref[:, sl] = (o / den).astype(BF16)


def _neighbourhood_attention(qn, kn, proj, k_ctx, v_ctx, bias, layer, *, seq):
    m = qn.shape[0]
    batch = m // seq
    n_blocks = seq // NA_QBLK
    past = k_ctx.shape[2]
    kind = lambda i: jnp.where(i == 0, 0, jnp.where(i == n_blocks - 1, 2, 1))
    ctx = pl.BlockSpec((1, 1, past, D_ATT), lambda b, i: (b, layer, 0, 0))
    return pl.pallas_call(
        functools.partial(_na_attn_kernel, n_blocks=n_blocks),
        grid=(batch, n_blocks),
        in_specs=[
            pl.BlockSpec((NA_QBLK, D_ATT), lambda b, i: (b * n_blocks + i, 0)),
            pl.BlockSpec((seq, D_ATT), lambda b, i: (b, 0)),
            pl.BlockSpec((seq, D_ATT), lambda b, i: (b, 2)),
            ctx, ctx,
            pl.BlockSpec((1, 1, N_HEADS, NA_QBLK, NA_KBLK), lambda b, i: (layer, kind(i), 0, 0, 0)),
        ],
        out_specs=pl.BlockSpec((NA_QBLK, D_ATT), lambda b, i: (b * n_blocks + i, 0)),
        out_shape=jax.ShapeDtypeStruct((m, D_ATT), BF16),
        compiler_params=_params("arbitrary", "arbitrary"),
        name="neighbourhood_attention",
    )(qn, kn, proj, k_ctx, v_ctx, bias)


def _stage_padded(pad_ref, x):
    zeros = jnp.zeros((PAD, pad_ref.shape[1]), F32)
    rows = x.shape[0]
    pad_ref[0:PAD, :] = zeros
    pad_ref[PAD + rows:2 * PAD + rows, :] = zeros
    pad_ref[PAD:PAD + rows, :] = x


def _conv_taps(pad_ref, seg, w, left, isolated):
    base = seg * SEG_LEN
    t = lax.broadcasted_iota(jnp.int32, (SEG_LEN, 1), 0)
    acc = None
    for k in range(w.shape[0]):
        off = k - left
        xs = pad_ref[PAD + base + off:PAD + base + off + SEG_LEN, :]
        if isolated and off != 0:
            xs = jnp.where((t + off >= 0) & (t + off < SEG_LEN), xs, 0.0)
        term = xs * w[k:k + 1, :]
        acc = term if acc is None else acc + term
    return acc


def _gelu_tanh(x):
    return x * (0.5 * (1.0 + jnp.tanh(0.7978845608028654 * (x + 0.044715 * (x * x * x)))))


def _lru_kernel(lxy_ref, h0_ref, cw_ref, cb_ref, w_ref, b_ref, lam_ref, *rest, chained):
    if chained:
        y_ref, xpad, a_tm, u_tm, p_tm = rest
    else:
        y_ref, fin_ref, xpad, a_tm, u_tm = rest
    last = (SEG_LEN - 1) * SEG_PER_BLOCK
    _stage_padded(xpad, lxy_ref[:, 0:D_LRU].astype(F32))
    cw = cw_ref[0]
    neg_lam = -lam_ref[0]
    softplus = jnp.maximum(neg_lam, 0.0) + jnp.log1p(jnp.exp(-jnp.abs(neg_lam)))
    decay = (-0.5 * LRU_C) * softplus
    for c in range(SEG_PER_BLOCK):
        xc = cb_ref[0] + _conv_taps(xpad, c, cw, LRU_CONV // 2, not chained)
        seg_rows = pl.ds(c, SEG_LEN, stride=SEG_PER_BLOCK)
        for n in range(LRU_BLOCKS):
            cols = slice(n * LRU_BW, (n + 1) * LRU_BW)
            xb = xc[:, cols]
            g = _dot(xb.astype(BF16), w_ref[0, n].astype(BF16)) + b_ref[0, n:n + 1, :]
            x_half = 0.5 * xb
            for d in range(2):
                ta = jnp.tanh(g[:, (2 * d) * LRU_BW:(2 * d + 1) * LRU_BW])
                ti = jnp.tanh(g[:, (2 * d + 1) * LRU_BW:(2 * d + 2) * LRU_BW])
                log_a = decay[d:d + 1, cols] * (1.0 + ta)
                th = jnp.tanh(log_a)
                p = -2.0 * th
                coef = p * lax.rsqrt(jnp.maximum(p * (1.0 - th), F32_MIN_NORMAL))
                a_tm[d, n, seg_rows, :] = jnp.exp(log_a)
                u_tm[d, n, seg_rows, :] = coef * ((1.0 + ti) * x_half)

    def scan_dir(d, rows, h, p):
        h_new, p_new = [], []
        for n in range(LRU_BLOCKS):
            a = a_tm[d, n, rows, :]
            hn = a * h[n] + u_tm[d, n, rows, :]
            u_tm[d, n, rows, :] = hn
            h_new.append(hn)
            if chained:
                pn = a * p[n]
                p_tm[d, n, rows, :] = pn
                p_new.append(pn)
        return tuple(h_new), tuple(p_new)

    def step(t, carry):
        hf, pf, hb, pb = carry
        hf, pf = scan_dir(0, pl.ds(pl.multiple_of(t * SEG_PER_BLOCK, SEG_PER_BLOCK), SEG_PER_BLOCK), hf, pf)
        tb = SEG_LEN - 1 - t
        hb, pb = scan_dir(1, pl.ds(pl.multiple_of(tb * SEG_PER_BLOCK, SEG_PER_BLOCK), SEG_PER_BLOCK), hb, pb)
        return hf, pf, hb, pb

    lane_blocks = [slice(n * LRU_BW, (n + 1) * LRU_BW) for n in range(LRU_BLOCKS)]
    if chained:
        zero = tuple(jnp.zeros((SEG_PER_BLOCK, LRU_BW), F32) for _ in lane_blocks)
        one = tuple(jnp.ones((SEG_PER_BLOCK, LRU_BW), F32) for _ in lane_blocks)
        init = (zero, one, zero, one)
    else:
        init = (tuple(h0_ref[0, :, cols] for cols in lane_blocks), (),
                tuple(h0_ref[1, :, cols] for cols in lane_blocks), ())
    lax.fori_loop(0, SEG_LEN, step, init, unroll=4)

    carry_f = carry_b = None
    if chained:
        carry_f, carry_b = [], []
        for n, cols in enumerate(lane_blocks):
            end_h, end_p = u_tm[0, n, last:last + SEG_PER_BLOCK, :], p_tm[0, n, last:last + SEG_PER_BLOCK, :]
            cf = [h0_ref[0, 0:1, cols]]
            for s in range(SEG_PER_BLOCK - 1):
                cf.append(end_h[s:s + 1] + end_p[s:s + 1] * cf[s])
            carry_f.append(cf)
            beg_h, beg_p = u_tm[1, n, 0:SEG_PER_BLOCK, :], p_tm[1, n, 0:SEG_PER_BLOCK, :]
            cb = [None] * SEG_PER_BLOCK
            cb[SEG_PER_BLOCK - 1] = h0_ref[0, 1:2, cols]
            for s in range(SEG_PER_BLOCK - 1, 0, -1):
                cb[s - 1] = beg_h[s:s + 1] + beg_p[s:s + 1] * cb[s]
            carry_b.append(cb)
    else:
        for n, cols in enumerate(lane_blocks):
            fin_ref[0, :, cols] = u_tm[0, n, last:last + SEG_PER_BLOCK, :]
            fin_ref[1, :, cols] = u_tm[1, n, 0:SEG_PER_BLOCK, :]

    for c in range(SEG_PER_BLOCK):
        seg_rows = pl.ds(c, SEG_LEN, stride=SEG_PER_BLOCK)
        tok = slice(c * SEG_LEN, (c + 1) * SEG_LEN)
        for n, cols in enumerate(lane_blocks):
            h = u_tm[0, n, seg_rows, :] + u_tm[1, n, seg_rows, :]
            if chained:
                h = h + p_tm[0, n, seg_rows, :] * carry_f[n][c] + p_tm[1, n, seg_rows, :] * carry_b[n][c]
            ly = lxy_ref[tok, D_LRU + n * LRU_BW:D_LRU + (n + 1) * LRU_BW].astype(F32)
            y_ref[tok, cols] = (h * _gelu_tanh(ly)).astype(BF16)


def _rglru(proj, h0, conv_w, conv_b, w_cat, b_cat, lam, layer, *, chained):
    m = proj.shape[0]
    blocks = m // MIX_ROWS
    lxy_col = 3 * D_ATT // (2 * D_LRU)
    per_layer = lambda *shape: pl.BlockSpec((1,) + shape, lambda b: (layer,) + (0,) * len(shape))
    tm_buf = pltpu.VMEM((2, LRU_BLOCKS, MIX_ROWS, LRU_BW), F32)
    if chained:
        h0_spec = pl.BlockSpec((1, 2, D_LRU), lambda b: (b, 0, 0))
        out_specs = [pl.BlockSpec((MIX_ROWS, D_LRU), lambda b: (b, 0))]
        out_shape = [jax.ShapeDtypeStruct((m, D_LRU), BF16)]
        scratch = [tm_buf, tm_buf, tm_buf]
    else:
        h0_spec = pl.BlockSpec((2, SEG_PER_BLOCK, D_LRU), lambda b: (0, b, 0))
        out_specs = [pl.BlockSpec((MIX_ROWS, D_LRU), lambda b: (b, 0)),
                     pl.BlockSpec((2, SEG_PER_BLOCK, D_LRU), lambda b: (0, b, 0))]
        out_shape = [jax.ShapeDtypeStruct((m, D_LRU), BF16),
                     jax.ShapeDtypeStruct((2, blocks * SEG_PER_BLOCK, D_LRU), F32)]
        scratch = [tm_buf, tm_buf]
    return pl.pallas_call(
        functools.partial(_lru_kernel, chained=chained),
        grid=(blocks,),
        in_specs=[
            pl.BlockSpec((MIX_ROWS, 2 * D_LRU), lambda b: (b, lxy_col)),
            h0_spec,
            per_layer(LRU_CONV, D_LRU), per_layer(1, D_LRU),
            per_layer(LRU_BLOCKS, LRU_BW, 4 * LRU_BW), per_layer(LRU_BLOCKS, 4 * LRU_BW),
            per_layer(2, D_LRU),
        ],
        out_specs=out_specs,
        out_shape=out_shape,
        scratch_shapes=[pltpu.VMEM((MIX_ROWS + 2 * PAD, D_LRU), F32)] + scratch,
        compiler_params=_params("arbitrary"),
        name="rglru",
    )(proj, h0, conv_w, conv_b.reshape(conv_b.shape[0], 1, D_LRU), w_cat, b_cat, lam)


def _sconv_kernel(bc_ref, x_ref, w_ref, y_ref, pad, *, isolated):
    _stage_padded(pad, bc_ref[:, D_SC:2 * D_SC].astype(F32) * x_ref[...].astype(F32))
    w = w_ref[0]
    for c in range(SEG_PER_BLOCK):
        tok = slice(c * SEG_LEN, (c + 1) * SEG_LEN)
        conv = _conv_taps(pad, c, w, SC_CONV // 2, isolated)
        y_ref[tok, :] = (bc_ref[tok, 0:D_SC].astype(F32) * conv).astype(BF16)


def _short_conv(proj, w, layer, *, isolated):
    m = proj.shape[0]
    col0 = 3 * D_ATT + 2 * D_LRU
    return pl.pallas_call(
        functools.partial(_sconv_kernel, isolated=isolated),
        grid=(m // MIX_ROWS,),
        in_specs=[pl.BlockSpec((MIX_ROWS, 2 * D_SC), lambda b: (b, col0 // (2 * D_SC))),
                  pl.BlockSpec((MIX_ROWS, D_SC), lambda b: (b, col0 // D_SC + 2)),
                  pl.BlockSpec((1, SC_CONV, D_SC), lambda b: (layer, 0, 0))],
        out_specs=pl.BlockSpec((MIX_ROWS, D_SC), lambda b: (b, 0)),
        out_shape=jax.ShapeDtypeStruct((m, D_SC), BF16),
        scratch_shapes=[pltpu.VMEM((MIX_ROWS + 2 * PAD, D_SC), F32)],
        compiler_params=_params("arbitrary"),
        name="short_conv",
    )(proj, proj, w)


def _lru_gate_weights(w_a, b_a, w_i, b_i):
    depth = w_a.shape[0]
    w_cat = jnp.concatenate([w_a[:, 0], w_i[:, 0], w_a[:, 1], w_i[:, 1]], axis=-1)
    blocks = lambda b: b.reshape(depth, 2, LRU_BLOCKS, LRU_BW)
    ba, bi = blocks(b_a), blocks(b_i)
    b_cat = jnp.concatenate([ba[:, 0], bi[:, 0], ba[:, 1], bi[:, 1]], axis=-1)
    return 0.5 * w_cat, 0.5 * b_cat


def _stream_layer(h, mod, p, layer, *, rows_per_mod, ctx, caches):
    m = h.shape[0]
    proj = _norm_matmul(h, mod, p["g_mix"], [p["w_in"]], layer, shift_idx=0, scale_idx=1,
                        rows_per_mod=rows_per_mod, tn=512, name="in_proj")
    if ctx is None:
        k_acc, v_acc = caches
        att, k_acc, v_acc = _context_attention(proj, p["q_gain"], p["k_gain"], layer, k_acc, v_acc,
                                               seq=SEG_LEN)
        caches = (k_acc, v_acc)
        h0 = jnp.zeros((2, m // SEG_LEN, D_LRU), F32)
        lru, lru_final = _rglru(proj, h0, p["lru_conv_w"], p["lru_conv_b"], p["w_cat"], p["b_cat"],
                                p["lru_lambda"], layer, chained=False)
    else:
        k_ctx, v_ctx, h0, bias = ctx
        qn, kn = _qk_norm(proj, p["q_gain"], p["k_gain"], layer)
        att = _neighbourhood_attention(qn, kn, proj, k_ctx, v_ctx, bias, layer, seq=MIX_ROWS)
        (lru,) = _rglru(proj, h0, p["lru_conv_w"], p["lru_conv_b"], p["w_cat"], p["b_cat"],
                        p["lru_lambda"], layer, chained=True)
        lru_final = None
    conv = _short_conv(proj, p["sc_conv_w"], layer, isolated=ctx is None)
    h = _matmul_residual([att, lru, conv], p["w_out"], layer, h, mod, gate_idx=2,
                         rows_per_mod=rows_per_mod, tm=m // 2, tn=512, ca=512, name="out_proj")
    hid = _norm_matmul(h, mod, p["g_ffn"], [p["w_ffn_gate"], p["w_ffn_up"]], layer, shift_idx=3,
                       scale_idx=4, rows_per_mod=rows_per_mod, tn=256, name="ffn_up")
    h = _matmul_residual([hid], p["w_ffn_down"], layer, h, mod, gate_idx=5,
                         rows_per_mod=rows_per_mod, tm=m // 2, tn=256, ca=256, name="ffn_down")
    return h, caches, lru_final


def kernel(x_prompt, x_sample, cache_k, cache_v, state_lru, c, c_ctx, w_mod, b_mod, g_mix, g_ffn, w_in, q_gain, k_gain, na_bias, lru_conv_w, lru_conv_b, lru_w_a, lru_b_a, lru_w_i, lru_b_i, lru_lambda, sc_conv_w, w_out, w_ffn_gate, w_ffn_up, w_ffn_down):
    batch, seq, d = x_prompt.shape
    dec_batch, dec_seq, _ = x_sample.shape
    depth = w_mod.shape[0]
    past = cache_k.shape[2]
    assert seq == SEG_LEN and dec_seq == MIX_ROWS and batch % SEG_PER_BLOCK == 0

    cond = jnp.zeros((MOD_ROWS, d), F32).at[0].set(c_ctx).at[1:1 + dec_batch].set(c)
    mod_all = _modulation(cond, w_mod, b_mod).reshape(depth, MOD_ROWS, 6, d)
    bias_all = _na_bias(na_bias)
    w_cat, b_cat = _lru_gate_weights(lru_w_a, lru_b_a, lru_w_i, lru_b_i)
    p = {
        "g_mix": g_mix, "g_ffn": g_ffn, "w_in": w_in, "q_gain": q_gain, "k_gain": k_gain,
        "lru_conv_w": lru_conv_w, "lru_conv_b": lru_conv_b, "w_cat": w_cat, "b_cat": b_cat,
        "lru_lambda": lru_lambda, "sc_conv_w": sc_conv_w, "w_out": w_out,
        "w_ffn_gate": w_ffn_gate, "w_ffn_up": w_ffn_up, "w_ffn_down": w_ffn_down,
    }
    k_ctx = cache_k.reshape(dec_batch, depth, past, D_ATT)
    v_ctx = cache_v.reshape(dec_batch, depth, past, D_ATT)

    hp = x_prompt.reshape(batch * seq, d)
    hs = x_sample.reshape(dec_batch * dec_seq, d)
    caches = (jnp.zeros((batch, depth, seq, D_ATT), F32), jnp.zeros((batch, depth, seq, D_ATT), F32))
    new_s = []
    for l in range(depth):
        hp, caches, s_l = _stream_layer(hp, mod_all[l, 0:1], p, l, rows_per_mod=batch * seq,
                                        ctx=None, caches=caches)
        new_s.append(jnp.swapaxes(s_l, 0, 1))
        ctx = (k_ctx, v_ctx, state_lru[:, l], bias_all)
        hs, _, _ = _stream_layer(hs, mod_all[l, 1:1 + dec_batch], p, l, rows_per_mod=dec_seq,
                                 ctx=ctx, caches=None)
    cache_shape = (batch, depth, seq, N_HEADS, HEAD_DIM)
    return (hp.reshape(batch, seq, d), hs.reshape(dec_batch, dec_seq, d),
            caches[0].reshape(cache_shape), caches[1].reshape(cache_shape), jnp.stack(new_s, axis=1))
```

```python
import functools

import jax
import jax.numpy as jnp
from jax import lax
from jax.experimental import pallas as pl
from jax.experimental.pallas import tpu as pltpu

D_MODEL = 2048
N_HEADS = 8
HEAD_DIM = 128
D_ATT = N_HEADS * HEAD_DIM
D_LRU = 512
LRU_BLOCKS = 4
LRU_BW = 128
LRU_CONV = 4
LRU_C = 8.0
D_SC = 512
SC_CONV = 3
GRID_W = 64
NA_ROWS = 8
NA_COLS = 16
EPS = 1e-6
LOG2_E = 1.4426950408889634
LOGIT_SCALE = HEAD_DIM ** -0.5 * LOG2_E

NA_QROWS = 4
NA_KROWS = 12
NA_QBLK = NA_QROWS * GRID_W
NA_KBLK = NA_KROWS * GRID_W
MASK_VALUE = -1e30

LANES = 128
SUBLANES = 8
MOD_ROWS = SUBLANES
VMEM_LIMIT_BYTES = 60 * 1024 * 1024

SEG_LEN = 256
SEG_PER_BLOCK = SUBLANES
MIX_ROWS = SEG_LEN * SEG_PER_BLOCK
PAD = SUBLANES
MM_ROWS = 1024

F32 = jnp.float32
BF16 = jnp.bfloat16
F32_MIN_NORMAL = 1.1754943508222875e-38


def _params(*semantics):
    return pltpu.CompilerParams(dimension_semantics=semantics,
                                vmem_limit_bytes=VMEM_LIMIT_BYTES)


def _dot(a, b):
    return jnp.dot(a, b, preferred_element_type=F32)


def _dot_nt(a, b):
    return lax.dot_general(a, b, (((1,), (1,)), ((), ())), preferred_element_type=F32)


def _sigmoid(x):
    return 0.5 * (1.0 + jnp.tanh(0.5 * x))


def _mod_kernel(c_ref, w_ref, b_ref, o_ref):
    c = c_ref[...]
    s = (c * _sigmoid(c)).astype(BF16)
    o_ref[0] = _dot(s, w_ref[0].astype(BF16)) + b_ref[0]


def _modulation(cond, w_mod, b_mod, *, tn=2048):
    depth, d, n = w_mod.shape
    return pl.pallas_call(
        _mod_kernel,
        grid=(depth, n // tn),
        in_specs=[
            pl.BlockSpec((MOD_ROWS, d), lambda l, j: (0, 0)),
            pl.BlockSpec((1, d, tn), lambda l, j: (l, 0, j)),
            pl.BlockSpec((1, 1, tn), lambda l, j: (l, 0, j)),
        ],
        out_specs=pl.BlockSpec((1, MOD_ROWS, tn), lambda l, j: (l, 0, j)),
        out_shape=jax.ShapeDtypeStruct((depth, MOD_ROWS, n), F32),
        compiler_params=_params("arbitrary", "arbitrary"),
        name="modulation",
    )(cond, w_mod, b_mod.reshape(depth, 1, n))


def _norm_mm_kernel(x_ref, mod_ref, g_ref, *rest, n_w, shift_idx, scale_idx, n_chunks):
    w_refs, o_ref, u_ref, wb_refs = rest[:n_w], rest[n_w], rest[n_w + 1], rest[n_w + 2:]
    s = pl.program_id(0)
    cm = x_ref.shape[0]
    slot = jnp.maximum(s - n_chunks, 0) % 2

    def cast_weights(into):
        for w_ref, wb_ref in zip(w_refs, wb_refs):
            wb_ref[into] = w_ref[0].astype(BF16)

    def matmul_rows(row0, rows, use):
        sl = pl.ds(pl.multiple_of(row0, rows), rows)
        u = u_ref[sl, :]
        a = _dot(u, wb_refs[0][use])
        if n_w == 2:
            a = a * _sigmoid(a) * _dot(u, wb_refs[1][use])
        o_ref[sl, :] = a.astype(o_ref.dtype)

    def norm_chunk():
        gain = g_ref[0] * (1.0 + mod_ref[0, scale_idx:scale_idx + 1, :])
        shift = mod_ref[0, shift_idx:shift_idx + 1, :]
        for p in range(cm // LANES):
            x = x_ref[p * LANES:(p + 1) * LANES, :]
            y = x * lax.rsqrt(jnp.mean(x * x, axis=-1, keepdims=True) + EPS)
            u_ref[pl.ds(pl.multiple_of(s * cm + p * LANES, LANES), LANES), :] = (y * gain + shift).astype(BF16)

    @pl.when(s == 0)
    def _():
        cast_weights(0)
        norm_chunk()

    @pl.when((s >= 1) & (s < n_chunks))
    def _():
        matmul_rows((s - 1) * cm, cm, 0)
        norm_chunk()

    @pl.when(s == n_chunks)
    def _():
        matmul_rows((n_chunks - 1) * cm, cm, 0)
        cast_weights(1)

    @pl.when(s > n_chunks)
    def _():
        for r in range(u_ref.shape[0] // MM_ROWS):
            matmul_rows(r * MM_ROWS, MM_ROWS, slot)
        cast_weights(1 - slot)


def _norm_matmul(x, mod, gain, weights, layer, *, shift_idx, scale_idx, rows_per_mod, tn, name,
                 cm=512):
    m, d = x.shape
    n = weights[0].shape[2]
    n_chunks = m // cm
    n_tiles = n // tn
    chunks_per_mod = rows_per_mod // cm
    chunk = lambda s: jnp.minimum(s, n_chunks - 1)
    tile = lambda s: jnp.maximum(s - n_chunks, 0)
    next_tile = lambda s: jnp.minimum(tile(s + 1), n_tiles - 1)
    kernel = functools.partial(_norm_mm_kernel, n_w=len(weights), shift_idx=shift_idx,
                               scale_idx=scale_idx, n_chunks=n_chunks)
    return pl.pallas_call(
        kernel,
        grid=(n_chunks + n_tiles,),
        in_specs=[
            pl.BlockSpec((cm, d), lambda s: (chunk(s), 0)),
            pl.BlockSpec((1, 6, d), lambda s: (chunk(s) // chunks_per_mod, 0, 0)),
            pl.BlockSpec((1, 1, d), lambda s: (layer, 0, 0)),
        ] + [pl.BlockSpec((1, d, tn), lambda s: (layer, 0, next_tile(s))) for _ in weights],
        out_specs=pl.BlockSpec((m, tn), lambda s: (0, tile(s))),
        out_shape=jax.ShapeDtypeStruct((m, n), BF16),
        scratch_shapes=[pltpu.VMEM((m, d), BF16)] + [pltpu.VMEM((2, d, tn), BF16) for _ in weights],
        compiler_params=_params("arbitrary"),
        name=name,
    )(x, mod, gain.reshape(gain.shape[0], 1, d), *weights)


def _mm_res_kernel(*refs, splits, gate_idx, rows_per_mod, n_chunks):
    n = len(splits)
    a_refs = refs[:n]
    w_ref, h_ref, mod_ref, o_ref, a_res, wb_ref = refs[n:]
    s = pl.program_id(1)
    tm = h_ref.shape[0]
    ca = a_refs[0].shape[0]
    tile_row0 = pl.program_id(0) * tm
    slot = jnp.maximum(s - n_chunks + 1, 0) % 2

    def cast_weights(into):
        wb_ref[into] = w_ref[0].astype(BF16)

    def product(row0, rows, use):
        sl = pl.ds(pl.multiple_of(row0, rows), rows)
        gate = mod_ref[(tile_row0 + row0) // rows_per_mod, gate_idx:gate_idx + 1, :]
        o_ref[sl, :] = h_ref[sl, :] + gate * _dot(a_res[sl, :], wb_ref[use])

    def stage_chunk():
        rows = pl.ds(pl.multiple_of(s * ca, ca), ca)
        k0 = 0
        for a_ref, kk in zip(a_refs, splits):
            a_res[rows, k0:k0 + kk] = a_ref[...]
            k0 += kk
        product(s * ca, ca, 0)

    @pl.when(s == 0)
    def _():
        cast_weights(0)
        stage_chunk()

    @pl.when((s >= 1) & (s < n_chunks - 1))
    def _():
        stage_chunk()

    @pl.when(s == n_chunks - 1)
    def _():
        stage_chunk()
        cast_weights(1)

    @pl.when(s >= n_chunks)
    def _():
        for r in range(tm // MM_ROWS):
            product(r * MM_ROWS, MM_ROWS, slot)
        cast_weights(1 - slot)


def _matmul_residual(a_parts, w, layer, h, mod, *, gate_idx, rows_per_mod, tm, tn, ca, name):
    m, n = h.shape
    k = w.shape[1]
    splits = tuple(a.shape[1] for a in a_parts)
    n_chunks = tm // ca
    n_tiles = n // tn
    assert n_chunks >= 2
    chunk = lambda i, s: i * n_chunks + jnp.minimum(s, n_chunks - 1)
    tile = lambda s: jnp.maximum(s - n_chunks + 1, 0)
    next_tile = lambda s: jnp.minimum(tile(s + 1), n_tiles - 1)
    kernel = functools.partial(_mm_res_kernel, splits=splits, gate_idx=gate_idx,
                               rows_per_mod=rows_per_mod, n_chunks=n_chunks)
    return pl.pallas_call(
        kernel,
        grid=(m // tm, n_chunks + n_tiles - 1),
        in_specs=[pl.BlockSpec((ca, kk), lambda i, s: (chunk(i, s), 0)) for kk in splits] + [
            pl.BlockSpec((1, k, tn), lambda i, s: (layer, 0, next_tile(s))),
            pl.BlockSpec((tm, tn), lambda i, s: (i, tile(s))),
            pl.BlockSpec((mod.shape[0], 6, tn), lambda i, s: (0, 0, tile(s))),
        ],
        out_specs=pl.BlockSpec((tm, tn), lambda i, s: (i, tile(s))),
        out_shape=jax.ShapeDtypeStruct((m, n), F32),
        scratch_shapes=[pltpu.VMEM((tm, k), BF16), pltpu.VMEM((2, k, tn), BF16)],
        compiler_params=_params("arbitrary", "arbitrary"),
        name=name,
    )(*a_parts, w, h, mod)


def _head_rms(x, gain):
    return x * lax.rsqrt(jnp.mean(x * x, axis=-1, keepdims=True) + EPS) * gain


def _ctx_attn_kernel(qkv_ref, qg_ref, kg_ref, kacc_ref, vacc_ref, att_ref, kout_ref, vout_ref):
    del kacc_ref, vacc_ref
    qg = qg_ref[0]
    kg = kg_ref[0]
    for h in range(N_HEADS):
        sl = slice(h * HEAD_DIM, (h + 1) * HEAD_DIM)
        qn = _head_rms(qkv_ref[:, sl].astype(F32), qg)
        kn = _head_rms(qkv_ref[:, D_ATT + h * HEAD_DIM:D_ATT + (h + 1) * HEAD_DIM].astype(F32), kg)
        v = qkv_ref[:, 2 * D_ATT + h * HEAD_DIM:2 * D_ATT + (h + 1) * HEAD_DIM]
        kout_ref[0, 0, :, sl] = kn
        vout_ref[0, 0, :, sl] = v.astype(F32)
        s = _dot_nt(qn.astype(BF16), kn.astype(BF16)) * LOGIT_SCALE
        e = jnp.exp2(s - jnp.max(s, axis=-1, keepdims=True))
        den = jnp.sum(e, axis=-1, keepdims=True)
        att_ref[:, sl] = (_dot(e.astype(BF16), v) / den).astype(BF16)


def _context_attention(proj, q_gain, k_gain, layer, k_acc, v_acc, *, seq):
    m = proj.shape[0]
    gain = pl.BlockSpec((1, 1, HEAD_DIM), lambda b: (layer, 0, 0))
    cache = pl.BlockSpec((1, 1, seq, D_ATT), lambda b: (b, layer, 0, 0))
    hbm = pl.BlockSpec(memory_space=pl.ANY)
    gains = lambda g: g.reshape(g.shape[0], 1, HEAD_DIM)
    return pl.pallas_call(
        _ctx_attn_kernel,
        grid=(m // seq,),
        in_specs=[pl.BlockSpec((seq, 3 * D_ATT), lambda b: (b, 0)), gain, gain, hbm, hbm],
        out_specs=[pl.BlockSpec((seq, D_ATT), lambda b: (b, 0)), cache, cache],
        out_shape=[jax.ShapeDtypeStruct((m, D_ATT), BF16),
                   jax.ShapeDtypeStruct(k_acc.shape, F32),
                   jax.ShapeDtypeStruct(v_acc.shape, F32)],
        input_output_aliases={3: 1, 4: 2},
        compiler_params=_params("arbitrary"),
        name="context_attention",
    )(proj, gains(q_gain), gains(k_gain), k_acc, v_acc)


def _qk_norm_kernel(qk_ref, qg_ref, kg_ref, qo_ref, ko_ref):
    qg = qg_ref[0]
    kg = kg_ref[0]
    for h in range(N_HEADS):
        sl = slice(h * HEAD_DIM, (h + 1) * HEAD_DIM)
        qo_ref[:, sl] = _head_rms(qk_ref[:, sl].astype(F32), qg).astype(BF16)
        k = qk_ref[:, D_ATT + h * HEAD_DIM:D_ATT + (h + 1) * HEAD_DIM]
        ko_ref[:, sl] = _head_rms(k.astype(F32), kg).astype(BF16)


def _qk_norm(proj, q_gain, k_gain, layer, *, tm=1024):
    m = proj.shape[0]
    gain = pl.BlockSpec((1, 1, HEAD_DIM), lambda i: (layer, 0, 0))
    out = pl.BlockSpec((tm, D_ATT), lambda i: (i, 0))
    gains = lambda g: g.reshape(g.shape[0], 1, HEAD_DIM)
    return pl.pallas_call(
        _qk_norm_kernel,
        grid=(m // tm,),
        in_specs=[pl.BlockSpec((tm, 2 * D_ATT), lambda i: (i, 0)), gain, gain],
        out_specs=[out, out],
        out_shape=[jax.ShapeDtypeStruct((m, D_ATT), BF16)] * 2,
        compiler_params=_params("arbitrary"),
        name="qk_norm",
    )(proj, gains(q_gain), gains(k_gain))


def _na_block_geometry():
    rows = 32
    return ((0, 0), (NA_QROWS, 0), (rows - NA_QROWS, rows - NA_KROWS)), rows


def _na_bias_kernel(rel_ref, o_ref):
    qc = lax.broadcasted_iota(jnp.int32, (GRID_W, GRID_W), 0)
    kc = lax.broadcasted_iota(jnp.int32, (GRID_W, GRID_W), 1)
    cs = jnp.clip(qc - NA_COLS // 2, 0, GRID_W - NA_COLS)
    col_ok = (kc >= cs) & (kc < cs + NA_COLS)
    dc = kc - qc + (NA_COLS - 1)
    hits = [dc == j for j in range(2 * NA_COLS - 1)]
    masked = jnp.full((GRID_W, GRID_W), MASK_VALUE, F32)
    tiles = []
    for dr in range(2 * NA_ROWS - 1):
        x = masked
        for j, hit in enumerate(hits):
            x = jnp.where(hit, rel_ref[0, 0, dr, j] * LOG2_E, x)
        tiles.append(jnp.where(col_ok, x, MASK_VALUE))
    kinds, rows = _na_block_geometry()
    for kind, (r0, kb) in enumerate(kinds):
        for qr in range(NA_QROWS):
            r = r0 + qr
            rs = min(max(r - NA_ROWS // 2, 0), rows - NA_ROWS)
            for kr in range(NA_KROWS):
                krow = kb + kr
                tile = tiles[krow - r + NA_ROWS - 1] if rs <= krow < rs + NA_ROWS else masked
                o_ref[0, kind, 0, qr * GRID_W:(qr + 1) * GRID_W,
                      kr * GRID_W:(kr + 1) * GRID_W] = tile


def _na_bias(na_bias):
    depth = na_bias.shape[0]
    return pl.pallas_call(
        _na_bias_kernel,
        grid=(depth, N_HEADS),
        in_specs=[pl.BlockSpec((1, 1) + na_bias.shape[2:], lambda l, h: (l, h, 0, 0),
                               memory_space=pltpu.SMEM)],
        out_specs=pl.BlockSpec((1, 3, 1, NA_QBLK, NA_KBLK), lambda l, h: (l, 0, h, 0, 0)),
        out_shape=jax.ShapeDtypeStruct((depth, 3, N_HEADS, NA_QBLK, NA_KBLK), F32),
        compiler_params=_params("arbitrary", "arbitrary"),
        name="na_bias",
    )(na_bias)


def _na_attn_kernel(q_ref, k_ref, v_ref, kc_ref, vc_ref, bias_ref, o_ref, *, n_blocks):
    blk = pl.program_id(1)
    start = pl.multiple_of(jnp.clip(blk - 1, 0, n_blocks - 3) * NA_QBLK, NA_QBLK)
    for h in range(N_HEADS):
        sl = slice(h * HEAD_DIM, (h + 1) * HEAD_DIM)
        q = q_ref[:, sl]
        kw = k_ref[pl.ds(start, NA_KBLK), sl]
        vw = v_ref[pl.ds(start, NA_KBLK), sl]
        s_loc = _dot_nt(q, kw) * LOGIT_SCALE + bias_ref[0, 0, h]
        s_ctx = _dot_nt(q, kc_ref[0, 0, :, h, :].astype(BF16)) * LOGIT_SCALE
        mx = jnp.maximum(jnp.max(s_loc, axis=-1, keepdims=True),
                         jnp.max(s_ctx, axis=-1, keepdims=True))
        e_loc = jnp.exp2(s_loc - mx)
        e_ctx = jnp.exp2(s_ctx - mx)
        den = jnp.sum(e_loc, axis=-1, keepdims=True) + jnp.sum(e_ctx, axis=-1, keepdims=True)
        o = _dot(e_loc.astype(BF16), vw) + _dot(e_ctx.astype(BF16), vc_ref[0, 0, :, h, :].astype(BF16))
        o_ref[:, sl] = (o / den).astype(BF16)


def _neighbourhood_attention(qn, kn, proj, k_ctx, v_ctx, bias, layer, *, seq):
    m = qn.shape[0]
    batch = m // seq
    n_blocks = seq // NA_QBLK
    past = k_ctx.shape[2]
    kind = lambda i: jnp.where(i == 0, 0, jnp.where(i == n_blocks - 1, 2, 1))
    ctx = pl.BlockSpec((1, 1, past, N_HEADS, HEAD_DIM), lambda b, i: (b, layer, 0, 0, 0))
    return pl.pallas_call(
        functools.partial(_na_attn_kernel, n_blocks=n_blocks),
        grid=(batch, n_blocks),
        in_specs=[
            pl.BlockSpec((NA_QBLK, D_ATT), lambda b, i: (b * n_blocks + i, 0)),
            pl.BlockSpec((seq, D_ATT), lambda b, i: (b, 0)),
            pl.BlockSpec((seq, D_ATT), lambda b, i: (b, 2)),
            ctx, ctx,
            pl.BlockSpec((1, 1, N_HEADS, NA_QBLK, NA_KBLK), lambda b, i: (layer, kind(i), 0, 0, 0)),
        ],
        out_specs=pl.BlockSpec((NA_QBLK, D_ATT), lambda b, i: (b * n_blocks + i, 0)),
        out_shape=jax.ShapeDtypeStruct((m, D_ATT), BF16),
        compiler_params=_params("arbitrary", "arbitrary"),
        name="neighbourhood_attention",
    )(qn, kn, proj, k_ctx, v_ctx, bias)


def _stage_padded(pad_ref, x):
    zeros = jnp.zeros((PAD, pad_ref.shape[1]), F32)
    rows = x.shape[0]
    pad_ref[0:PAD, :] = zeros
    pad_ref[PAD + rows:2 * PAD + rows, :] = zeros
    pad_ref[PAD:PAD + rows, :] = x


def _conv_taps(pad_ref, seg, w, left, isolated):
    base = seg * SEG_LEN
    t = lax.broadcasted_iota(jnp.int32, (SEG_LEN, 1), 0)
    acc = None
    for k in range(w.shape[0]):
        off = k - left
        xs = pad_ref[PAD + base + off:PAD + base + off + SEG_LEN, :]
        if isolated and off != 0:
            xs = jnp.where((t + off >= 0) & (t + off < SEG_LEN), xs, 0.0)
        term = xs * w[k:k + 1, :]
        acc = term if acc is None else acc + term
    return acc


def _gelu_tanh(x):
    return x * (0.5 * (1.0 + jnp.tanh(0.7978845608028654 * (x + 0.044715 * (x * x * x)))))


def _lru_kernel(lxy_ref, h0_ref, cw_ref, cb_ref, w_ref, b_ref, lam_ref, *rest, chained):
    if chained:
        y_ref, xpad, a_tm, u_tm, p_tm = rest
    else:
        y_ref, fin_ref, xpad, a_tm, u_tm = rest
    last = (SEG_LEN - 1) * SEG_PER_BLOCK
    _stage_padded(xpad, lxy_ref[:, 0:D_LRU].astype(F32))
    cw = cw_ref[0]
    neg_lam = -lam_ref[0]
    softplus = jnp.maximum(neg_lam, 0.0) + jnp.log1p(jnp.exp(-jnp.abs(neg_lam)))
    decay = (-0.5 * LRU_C) * softplus
    for c in range(SEG_PER_BLOCK):
        xc = cb_ref[0] + _conv_taps(xpad, c, cw, LRU_CONV // 2, not chained)
        seg_rows = pl.ds(c, SEG_LEN, stride=SEG_PER_BLOCK)
        for n in range(LRU_BLOCKS):
            cols = slice(n * LRU_BW, (n + 1) * LRU_BW)
            xb = xc[:, cols]
            g = _dot(xb.astype(BF16), w_ref[0, n].astype(BF16)) + b_ref[0, n:n + 1, :]
            x_half = 0.5 * xb
            for d in range(2):
                ta = jnp.tanh(g[:, (2 * d) * LRU_BW:(2 * d + 1) * LRU_BW])
                ti = jnp.tanh(g[:, (2 * d + 1) * LRU_BW:(2 * d + 2) * LRU_BW])
                log_a = decay[d:d + 1, cols] * (1.0 + ta)
                th = jnp.tanh(log_a)
                p = -2.0 * th
                coef = p * lax.rsqrt(jnp.maximum(p * (1.0 - th), F32_MIN_NORMAL))
                a_tm[d, n, seg_rows, :] = jnp.exp(log_a)
                u_tm[d, n, seg_rows, :] = coef * ((1.0 + ti) * x_half)

    def scan_dir(d, rows, h, p):
        h_new, p_new = [], []
        for n in range(LRU_BLOCKS):
            a = a_tm[d, n, rows, :]
            hn = a * h[n] + u_tm[d, n, rows, :]
            u_tm[d, n, rows, :] = hn
            h_new.append(hn)
            if chained:
                pn = a * p[n]
                p_tm[d, n, rows, :] = pn
                p_new.append(pn)
        return tuple(h_new), tuple(p_new)

    def step(t, carry):
        hf, pf, hb, pb = carry
        hf, pf = scan_dir(0, pl.ds(pl.multiple_of(t * SEG_PER_BLOCK, SEG_PER_BLOCK), SEG_PER_BLOCK), hf, pf)
        tb = SEG_LEN - 1 - t
        hb, pb = scan_dir(1, pl.ds(pl.multiple_of(tb * SEG_PER_BLOCK, SEG_PER_BLOCK), SEG_PER_BLOCK), hb, pb)
        return hf, pf, hb, pb

    lane_blocks = [slice(n * LRU_BW, (n + 1) * LRU_BW) for n in range(LRU_BLOCKS)]
    if chained:
        zero = tuple(jnp.zeros((SEG_PER_BLOCK, LRU_BW), F32) for _ in lane_blocks)
        one = tuple(jnp.ones((SEG_PER_BLOCK, LRU_BW), F32) for _ in lane_blocks)
        init = (zero, one, zero, one)
    else:
        init = (tuple(h0_ref[0, :, cols] for cols in lane_blocks), (),
                tuple(h0_ref[1, :, cols] for cols in lane_blocks), ())
    lax.fori_loop(0, SEG_LEN, step, init, unroll=4)

    carry_f = carry_b = None
    if chained:
        carry_f, carry_b = [], []
        for n, cols in enumerate(lane_blocks):
            end_h, end_p = u_tm[0, n, last:last + SEG_PER_BLOCK, :], p_tm[0, n, last:last + SEG_PER_BLOCK, :]
            cf = [h0_ref[0, 0:1, cols]]
            for s in range(SEG_PER_BLOCK - 1):
                cf.append(end_h[s:s + 1] + end_p[s:s + 1] * cf[s])
            carry_f.append(cf)
            beg_h, beg_p = u_tm[1, n, 0:SEG_PER_BLOCK, :], p_tm[1, n, 0:SEG_PER_BLOCK, :]
            cb = [None] * SEG_PER_BLOCK
            cb[SEG_PER_BLOCK - 1] = h0_ref[0, 1:2, cols]
            for s in range(SEG_PER_BLOCK - 1, 0, -1):
                cb[s - 1] = beg_h[s:s + 1] + beg_p[s:s + 1] * cb[s]
            carry_b.append(cb)
    else:
        for n, cols in enumerate(lane_blocks):
            fin_ref[0, :, cols] = u_tm[0, n, last:last + SEG_PER_BLOCK, :]
            fin_ref[1, :, cols] = u_tm[1, n, 0:SEG_PER_BLOCK, :]

    for c in range(SEG_PER_BLOCK):
        seg_rows = pl.ds(c, SEG_LEN, stride=SEG_PER_BLOCK)
        tok = slice(c * SEG_LEN, (c + 1) * SEG_LEN)
        for n, cols in enumerate(lane_blocks):
            h = u_tm[0, n, seg_rows, :] + u_tm[1, n, seg_rows, :]
            if chained:
                h = h + p_tm[0, n, seg_rows, :] * carry_f[n][c] + p_tm[1, n, seg_rows, :] * carry_b[n][c]
            ly = lxy_ref[tok, D_LRU + n * LRU_BW:D_LRU + (n + 1) * LRU_BW].astype(F32)
            y_ref[tok, cols] = (h * _gelu_tanh(ly)).astype(BF16)


def _rglru(proj, h0, conv_w, conv_b, w_cat, b_cat, lam, layer, *, chained):
    m = proj.shape[0]
    blocks = m // MIX_ROWS
    lxy_col = 3 * D_ATT // (2 * D_LRU)
    per_layer = lambda *shape: pl.BlockSpec((1,) + shape, lambda b: (layer,) + (0,) * len(shape))
    tm_buf = pltpu.VMEM((2, LRU_BLOCKS, MIX_ROWS, LRU_BW), F32)
    if chained:
        h0_spec = pl.BlockSpec((1, 2, D_LRU), lambda b: (b, 0, 0))
        out_specs = [pl.BlockSpec((MIX_ROWS, D_LRU), lambda b: (b, 0))]
        out_shape = [jax.ShapeDtypeStruct((m, D_LRU), BF16)]
        scratch = [tm_buf, tm_buf, tm_buf]
    else:
        h0_spec = pl.BlockSpec((2, SEG_PER_BLOCK, D_LRU), lambda b: (0, b, 0))
        out_specs = [pl.BlockSpec((MIX_ROWS, D_LRU), lambda b: (b, 0)),
                     pl.BlockSpec((2, SEG_PER_BLOCK, D_LRU), lambda b: (0, b, 0))]
        out_shape = [jax.ShapeDtypeStruct((m, D_LRU), BF16),
                     jax.ShapeDtypeStruct((2, blocks * SEG_PER_BLOCK, D_LRU), F32)]
        scratch = [tm_buf, tm_buf]
    return pl.pallas_call(
        functools.partial(_lru_kernel, chained=chained),
        grid=(blocks,),
        in_specs=[
            pl.BlockSpec((MIX_ROWS, 2 * D_LRU), lambda b: (b, lxy_col)),
            h0_spec,
            per_layer(LRU_CONV, D_LRU), per_layer(1, D_LRU),
            per_layer(LRU_BLOCKS, LRU_BW, 4 * LRU_BW), per_layer(LRU_BLOCKS, 4 * LRU_BW),
            per_layer(2, D_LRU),
        ],
        out_specs=out_specs,
        out_shape=out_shape,
        scratch_shapes=[pltpu.VMEM((MIX_ROWS + 2 * PAD, D_LRU), F32)] + scratch,
        compiler_params=_params("arbitrary"),
        name="rglru",
    )(proj, h0, conv_w, conv_b.reshape(conv_b.shape[0], 1, D_LRU), w_cat, b_cat, lam)


def _sconv_kernel(bc_ref, x_ref, w_ref, y_ref, pad, *, isolated):
    _stage_padded(pad, bc_ref[:, D_SC:2 * D_SC].astype(F32) * x_ref[...].astype(F32))
    w = w_ref[0]
    for c in range(SEG_PER_BLOCK):
        tok = slice(c * SEG_LEN, (c + 1) * SEG_LEN)
        conv = _conv_taps(pad, c, w, SC_CONV // 2, isolated)
        y_ref[tok, :] = (bc_ref[tok, 0:D_SC].astype(F32) * conv).astype(BF16)


def _short_conv(proj, w, layer, *, isolated):
    m = proj.shape[0]
    col0 = 3 * D_ATT + 2 * D_LRU
    return pl.pallas_call(
        functools.partial(_sconv_kernel, isolated=isolated),
        grid=(m // MIX_ROWS,),
        in_specs=[pl.BlockSpec((MIX_ROWS, 2 * D_SC), lambda b: (b, col0 // (2 * D_SC))),
                  pl.BlockSpec((MIX_ROWS, D_SC), lambda b: (b, col0 // D_SC + 2)),
                  pl.BlockSpec((1, SC_CONV, D_SC), lambda b: (layer, 0, 0))],
        out_specs=pl.BlockSpec((MIX_ROWS, D_SC), lambda b: (b, 0)),
        out_shape=jax.ShapeDtypeStruct((m, D_SC), BF16),
        scratch_shapes=[pltpu.VMEM((MIX_ROWS + 2 * PAD, D_SC), F32)],
        compiler_params=_params("arbitrary"),
        name="short_conv",
    )(proj, proj, w)


def _lru_gate_weights(w_a, b_a, w_i, b_i):
    depth = w_a.shape[0]
    w_cat = jnp.concatenate([w_a[:, 0], w_i[:, 0], w_a[:, 1], w_i[:, 1]], axis=-1)
    blocks = lambda b: b.reshape(depth, 2, LRU_BLOCKS, LRU_BW)
    ba, bi = blocks(b_a), blocks(b_i)
    b_cat = jnp.concatenate([ba[:, 0], bi[:, 0], ba[:, 1], bi[:, 1]], axis=-1)
    return 0.5 * w_cat, 0.5 * b_cat


def _stream_layer(h, mod, p, layer, *, rows_per_mod, ctx, caches):
    m = h.shape[0]
    proj = _norm_matmul(h, mod, p["g_mix"], [p["w_in"]], layer, shift_idx=0, scale_idx=1,
                        rows_per_mod=rows_per_mod, tn=512, name="in_proj")
    if ctx is None:
        k_acc, v_acc = caches
        att, k_acc, v_acc = _context_attention(proj, p["q_gain"], p["k_gain"], layer, k_acc, v_acc,
                                               seq=SEG_LEN)
        caches = (k_acc, v_acc)
        h0 = jnp.zeros((2, m // SEG_LEN, D_LRU), F32)
        lru, lru_final = _rglru(proj, h0, p["lru_conv_w"], p["lru_conv_b"], p["w_cat"], p["b_cat"],
                                p["lru_lambda"], layer, chained=False)
    else:
        k_ctx, v_ctx, h0, bias = ctx
        qn, kn = _qk_norm(proj, p["q_gain"], p["k_gain"], layer)
        att = _neighbourhood_attention(qn, kn, proj, k_ctx, v_ctx, bias, layer, seq=MIX_ROWS)
        (lru,) = _rglru(proj, h0, p["lru_conv_w"], p["lru_conv_b"], p["w_cat"], p["b_cat"],
                        p["lru_lambda"], layer, chained=True)
        lru_final = None
    conv = _short_conv(proj, p["sc_conv_w"], layer, isolated=ctx is None)
    h = _matmul_residual([att, lru, conv], p["w_out"], layer, h, mod, gate_idx=2,
                         rows_per_mod=rows_per_mod, tm=m, tn=256, ca=512, name="out_proj")
    hid = _norm_matmul(h, mod, p["g_ffn"], [p["w_ffn_gate"], p["w_ffn_up"]], layer, shift_idx=3,
                       scale_idx=4, rows_per_mod=rows_per_mod, tn=256, name="ffn_up")
    h = _matmul_residual([hid], p["w_ffn_down"], layer, h, mod, gate_idx=5,
                         rows_per_mod=rows_per_mod, tm=m // 2, tn=256, ca=256, name="ffn_down")
    return h, caches, lru_final


def kernel(x_prompt, x_sample, cache_k, cache_v, state_lru, c, c_ctx, w_mod, b_mod, g_mix, g_ffn, w_in, q_gain, k_gain, na_bias, lru_conv_w, lru_conv_b, lru_w_a, lru_b_a, lru_w_i, lru_b_i, lru_lambda, sc_conv_w, w_out, w_ffn_gate, w_ffn_up, w_ffn_down):
    batch, seq, d = x_prompt.shape
    dec_batch, dec_seq, _ = x_sample.shape
    depth = w_mod.shape[0]
    assert seq == SEG_LEN and dec_seq == MIX_ROWS and batch % SEG_PER_BLOCK == 0

    cond = jnp.zeros((MOD_ROWS, d), F32).at[0].set(c_ctx).at[1:1 + dec_batch].set(c)
    mod_all = _modulation(cond, w_mod, b_mod).reshape(depth, MOD_ROWS, 6, d)
    bias_all = _na_bias(na_bias)
    w_cat, b_cat = _lru_gate_weights(lru_w_a, lru_b_a, lru_w_i, lru_b_i)
    p = {
        "g_mix": g_mix, "g_ffn": g_ffn, "w_in": w_in, "q_gain": q_gain, "k_gain": k_gain,
        "lru_conv_w": lru_conv_w, "lru_conv_b": lru_conv_b, "w_cat": w_cat, "b_cat": b_cat,
        "lru_lambda": lru_lambda, "sc_conv_w": sc_conv_w, "w_out": w_out,
        "w_ffn_gate": w_ffn_gate, "w_ffn_up": w_ffn_up, "w_ffn_down": w_ffn_down,
    }

    hp = x_prompt.reshape(batch * seq, d)
    hs = x_sample.reshape(dec_batch * dec_seq, d)
    caches = (jnp.zeros((batch, depth, seq, D_ATT), F32), jnp.zeros((batch, depth, seq, D_ATT), F32))
    new_s = []
    for l in range(depth):
        hp, caches, s_l = _stream_layer(hp, mod_all[l, 0:1], p, l, rows_per_mod=batch * seq,
                                        ctx=None, caches=caches)
        new_s.append(jnp.swapaxes(s_l, 0, 1))
        ctx = (cache_k, cache_v, state_lru[:, l], bias_all)
        hs, _, _ = _stream_layer(hs, mod_all[l, 1:1 + dec_batch], p, l, rows_per_mod=dec_seq,
                                 ctx=ctx, caches=None)
    cache_shape = (batch, depth, seq, N_HEADS, HEAD_DIM)
    return (hp.reshape(batch, seq, d), hs.reshape(dec_batch, dec_seq, d),
            caches[0].reshape(cache_shape), caches[1].reshape(cache_shape), jnp.stack(new_s, axis=1))
```

```python
import functools

import jax
import jax.numpy as jnp
from jax import lax
from jax.experimental import pallas as pl
from jax.experimental.pallas import tpu as pltpu

D_MODEL = 2048
N_HEADS = 8
HEAD_DIM = 128
D_ATT = N_HEADS * HEAD_DIM
D_LRU = 512
LRU_BLOCKS = 4
LRU_BW = 128
LRU_CONV = 4
LRU_C = 8.0
D_SC = 512
SC_CONV = 3
GRID_W = 64
NA_ROWS = 8
NA_COLS = 16
EPS = 1e-6
LOG2_E = 1.4426950408889634
LOGIT_SCALE = HEAD_DIM ** -0.5 * LOG2_E

NA_QROWS = 4
NA_KROWS = 12
NA_QBLK = NA_QROWS * GRID_W
NA_KBLK = NA_KROWS * GRID_W
MASK_VALUE = -1e30

LANES = 128
SUBLANES = 8
MOD_ROWS = SUBLANES
VMEM_LIMIT_BYTES = 60 * 1024 * 1024

SEG_LEN = 256
SEG_PER_BLOCK = SUBLANES
MIX_ROWS = SEG_LEN * SEG_PER_BLOCK
PAD = SUBLANES
MM_ROWS = 1024

F32 = jnp.float32
BF16 = jnp.bfloat16
F32_MIN_NORMAL = 1.1754943508222875e-38


def _params(*semantics):
    return pltpu.CompilerParams(dimension_semantics=semantics,
                                vmem_limit_bytes=VMEM_LIMIT_BYTES)


def _dot(a, b):
    return jnp.dot(a, b, preferred_element_type=F32)


def _dot_nt(a, b):
    return lax.dot_general(a, b, (((1,), (1,)), ((), ())), preferred_element_type=F32)


def _sigmoid(x):
    return 0.5 * (1.0 + jnp.tanh(0.5 * x))


def _mod_kernel(c_ref, w_ref, b_ref, o_ref):
    c = c_ref[...]
    s = (c * _sigmoid(c)).astype(BF16)
    o_ref[0] = _dot(s, w_ref[0].astype(BF16)) + b_ref[0]


def _modulation(cond, w_mod, b_mod, *, tn=1024):
    depth, d, n = w_mod.shape
    return pl.pallas_call(
        _mod_kernel,
        grid=(depth, n // tn),
        in_specs=[
            pl.BlockSpec((MOD_ROWS, d), lambda l, j: (0, 0)),
            pl.BlockSpec((1, d, tn), lambda l, j: (l, 0, j)),
            pl.BlockSpec((1, 1, tn), lambda l, j: (l, 0, j)),
        ],
        out_specs=pl.BlockSpec((1, MOD_ROWS, tn), lambda l, j: (l, 0, j)),
        out_shape=jax.ShapeDtypeStruct((depth, MOD_ROWS, n), F32),
        compiler_params=_params("arbitrary", "arbitrary"),
        name="modulation",
    )(cond, w_mod, b_mod.reshape(depth, 1, n))


def _norm_mm_kernel(x_ref, mod_ref, g_ref, *rest, n_w, shift_idx, scale_idx, n_chunks):
    w_refs, o_ref, u_ref, wb_refs = rest[:n_w], rest[n_w], rest[n_w + 1], rest[n_w + 2:]
    s = pl.program_id(0)
    cm = x_ref.shape[0]
    slot = jnp.maximum(s - n_chunks, 0) % 2

    def cast_weights(into):
        for w_ref, wb_ref in zip(w_refs, wb_refs):
            wb_ref[into] = w_ref[0].astype(BF16)

    def matmul_rows(row0, rows, use):
        sl = pl.ds(pl.multiple_of(row0, rows), rows)
        u = u_ref[sl, :]
        a = _dot(u, wb_refs[0][use])
        if n_w == 2:
            a = a * _sigmoid(a) * _dot(u, wb_refs[1][use])
        o_ref[sl, :] = a.astype(o_ref.dtype)

    def norm_chunk():
        gain = g_ref[0] * (1.0 + mod_ref[0, scale_idx:scale_idx + 1, :])
        shift = mod_ref[0, shift_idx:shift_idx + 1, :]
        for p in range(cm // LANES):
            x = x_ref[p * LANES:(p + 1) * LANES, :]
            y = x * lax.rsqrt(jnp.mean(x * x, axis=-1, keepdims=True) + EPS)
            u_ref[pl.ds(pl.multiple_of(s * cm + p * LANES, LANES), LANES), :] = (y * gain + shift).astype(BF16)

    @pl.when(s == 0)
    def _():
        cast_weights(0)
        norm_chunk()

    @pl.when((s >= 1) & (s < n_chunks))
    def _():
        matmul_rows((s - 1) * cm, cm, 0)
        norm_chunk()

    @pl.when(s == n_chunks)
    def _():
        matmul_rows((n_chunks - 1) * cm, cm, 0)
        cast_weights(1)

    @pl.when(s > n_chunks)
    def _():
        for r in range(u_ref.shape[0] // MM_ROWS):
            matmul_rows(r * MM_ROWS, MM_ROWS, slot)
        cast_weights(1 - slot)


def _norm_matmul(x, mod, gain, weights, layer, *, shift_idx, scale_idx, rows_per_mod, tn, name,
                 cm=512):
    m, d = x.shape
    n = weights[0].shape[2]
    n_chunks = m // cm
    n_tiles = n // tn
    chunks_per_mod = rows_per_mod // cm
    chunk = lambda s: jnp.minimum(s, n_chunks - 1)
    tile = lambda s: jnp.maximum(s - n_chunks, 0)
    next_tile = lambda s: jnp.minimum(tile(s + 1), n_tiles - 1)
    kernel = functools.partial(_norm_mm_kernel, n_w=len(weights), shift_idx=shift_idx,
                               scale_idx=scale_idx, n_chunks=n_chunks)
    return pl.pallas_call(
        kernel,
        grid=(n_chunks + n_tiles,),
        in_specs=[
            pl.BlockSpec((cm, d), lambda s: (chunk(s), 0)),
            pl.BlockSpec((1, 6, d), lambda s: (chunk(s) // chunks_per_mod, 0, 0)),
            pl.BlockSpec((1, 1, d), lambda s: (layer, 0, 0)),
        ] + [pl.BlockSpec((1, d, tn), lambda s: (layer, 0, next_tile(s))) for _ in weights],
        out_specs=pl.BlockSpec((m, tn), lambda s: (0, tile(s))),
        out_shape=jax.ShapeDtypeStruct((m, n), BF16),
        scratch_shapes=[pltpu.VMEM((m, d), BF16)] + [pltpu.VMEM((2, d, tn), BF16) for _ in weights],
        compiler_params=_params("arbitrary"),
        name=name,
    )(x, mod, gain.reshape(gain.shape[0], 1, d), *weights)


def _mm_res_kernel(*refs, splits, gate_idx, rows_per_mod, n_chunks):
    n = len(splits)
    a_refs = refs[:n]
    w_ref, h_ref, mod_ref, o_ref, a_res, wb_ref = refs[n:]
    s = pl.program_id(1)
    tm = h_ref.shape[0]
    ca = a_refs[0].shape[0]
    tile_row0 = pl.program_id(0) * tm
    slot = jnp.maximum(s - n_chunks + 1, 0) % 2

    def cast_weights(into):
        wb_ref[into] = w_ref[0].astype(BF16)

    def product(row0, rows, use):
        sl = pl.ds(pl.multiple_of(row0, rows), rows)
        gate = mod_ref[(tile_row0 + row0) // rows_per_mod, gate_idx:gate_idx + 1, :]
        o_ref[sl, :] = h_ref[sl, :] + gate * _dot(a_res[sl, :], wb_ref[use])

    def stage_chunk():
        rows = pl.ds(pl.multiple_of(s * ca, ca), ca)
        k0 = 0
        for a_ref, kk in zip(a_refs, splits):
            a_res[rows, k0:k0 + kk] = a_ref[...]
            k0 += kk
        product(s * ca, ca, 0)

    @pl.when(s == 0)
    def _():
        cast_weights(0)
        stage_chunk()

    @pl.when((s >= 1) & (s < n_chunks - 1))
    def _():
        stage_chunk()

    @pl.when(s == n_chunks - 1)
    def _():
        stage_chunk()
        cast_weights(1)

    @pl.when(s >= n_chunks)
    def _():
        for r in range(tm // MM_ROWS):
            product(r * MM_ROWS, MM_ROWS, slot)
        cast_weights(1 - slot)


def _matmul_residual(a_parts, w, layer, h, mod, *, gate_idx, rows_per_mod, tm, tn, ca, name):
    m, n = h.shape
    k = w.shape[1]
    splits = tuple(a.shape[1] for a in a_parts)
    n_chunks = tm // ca
    n_tiles = n // tn
    assert n_chunks >= 2
    chunk = lambda i, s: i * n_chunks + jnp.minimum(s, n_chunks - 1)
    tile = lambda s: jnp.maximum(s - n_chunks + 1, 0)
    next_tile = lambda s: jnp.minimum(tile(s + 1), n_tiles - 1)
    kernel = functools.partial(_mm_res_kernel, splits=splits, gate_idx=gate_idx,
                               rows_per_mod=rows_per_mod, n_chunks=n_chunks)
    return pl.pallas_call(
        kernel,
        grid=(m // tm, n_chunks + n_tiles - 1),
        in_specs=[pl.BlockSpec((ca, kk), lambda i, s: (chunk(i, s), 0)) for kk in splits] + [
            pl.BlockSpec((1, k, tn), lambda i, s: (layer, 0, next_tile(s))),
            pl.BlockSpec((tm, tn), lambda i, s: (i, tile(s))),
            pl.BlockSpec((mod.shape[0], 6, tn), lambda i, s: (0, 0, tile(s))),
        ],
        out_specs=pl.BlockSpec((tm, tn), lambda i, s: (i, tile(s))),
        out_shape=jax.ShapeDtypeStruct((m, n), F32),
        scratch_shapes=[pltpu.VMEM((tm, k), BF16), pltpu.VMEM((2, k, tn), BF16)],
        compiler_params=_params("arbitrary", "arbitrary"),
        name=name,
    )(*a_parts, w, h, mod)


def _head_rms(x, gain):
    return x * lax.rsqrt(jnp.mean(x * x, axis=-1, keepdims=True) + EPS) * gain


def _ctx_attn_kernel(qkv_ref, qg_ref, kg_ref, kacc_ref, vacc_ref, att_ref, kout_ref, vout_ref):
    del kacc_ref, vacc_ref
    qg = qg_ref[0]
    kg = kg_ref[0]
    for h in range(N_HEADS):
        sl = slice(h * HEAD_DIM, (h + 1) * HEAD_DIM)
        qn = _head_rms(qkv_ref[:, sl].astype(F32), qg)
        kn = _head_rms(qkv_ref[:, D_ATT + h * HEAD_DIM:D_ATT + (h + 1) * HEAD_DIM].astype(F32), kg)
        v = qkv_ref[:, 2 * D_ATT + h * HEAD_DIM:2 * D_ATT + (h + 1) * HEAD_DIM]
        kout_ref[0, 0, :, sl] = kn
        vout_ref[0, 0, :, sl] = v.astype(F32)
        s = _dot_nt(qn.astype(BF16), kn.astype(BF16)) * LOGIT_SCALE
        e = jnp.exp2(s - jnp.max(s, axis=-1, keepdims=True))
        den = jnp.sum(e, axis=-1, keepdims=True)
        att_ref[:, sl] = (_dot(e.astype(BF16), v) / den).astype(BF16)


def _context_attention(proj, q_gain, k_gain, layer, k_acc, v_acc, *, seq):
    m = proj.shape[0]
    gain = pl.BlockSpec((1, 1, HEAD_DIM), lambda b: (layer, 0, 0))
    cache = pl.BlockSpec((1, 1, seq, D_ATT), lambda b: (b, layer, 0, 0))
    hbm = pl.BlockSpec(memory_space=pl.ANY)
    gains = lambda g: g.reshape(g.shape[0], 1, HEAD_DIM)
    return pl.pallas_call(
        _ctx_attn_kernel,
        grid=(m // seq,),
        in_specs=[pl.BlockSpec((seq, 3 * D_ATT), lambda b: (b, 0)), gain, gain, hbm, hbm],
        out_specs=[pl.BlockSpec((seq, D_ATT), lambda b: (b, 0)), cache, cache],
        out_shape=[jax.ShapeDtypeStruct((m, D_ATT), BF16),
                   jax.ShapeDtypeStruct(k_acc.shape, F32),
                   jax.ShapeDtypeStruct(v_acc.shape, F32)],
        input_output_aliases={3: 1, 4: 2},
        compiler_params=_params("arbitrary"),
        name="context_attention",
    )(proj, gains(q_gain), gains(k_gain), k_acc, v_acc)


def _qk_norm_kernel(qk_ref, qg_ref, kg_ref, qo_ref, ko_ref):
    qg = qg_ref[0]
    kg = kg_ref[0]
    for h in range(N_HEADS):
        sl = slice(h * HEAD_DIM, (h + 1) * HEAD_DIM)
        qo_ref[:, sl] = _head_rms(qk_ref[:, sl].astype(F32), qg).astype(BF16)
        k = qk_ref[:, D_ATT + h * HEAD_DIM:D_ATT + (h + 1) * HEAD_DIM]
        ko_ref[:, sl] = _head_rms(k.astype(F32), kg).astype(BF16)


def _qk_norm(proj, q_gain, k_gain, layer, *, tm=1024):
    m = proj.shape[0]
    gain = pl.BlockSpec((1, 1, HEAD_DIM), lambda i: (layer, 0, 0))
    out = pl.BlockSpec((tm, D_ATT), lambda i: (i, 0))
    gains = lambda g: g.reshape(g.shape[0], 1, HEAD_DIM)
    return pl.pallas_call(
        _qk_norm_kernel,
        grid=(m // tm,),
        in_specs=[pl.BlockSpec((tm, 2 * D_ATT), lambda i: (i, 0)), gain, gain],
        out_specs=[out, out],
        out_shape=[jax.ShapeDtypeStruct((m, D_ATT), BF16)] * 2,
        compiler_params=_params("arbitrary"),
        name="qk_norm",
    )(proj, gains(q_gain), gains(k_gain))


def _na_block_geometry():
    rows = 32
    return ((0, 0), (NA_QROWS, 0), (rows - NA_QROWS, rows - NA_KROWS)), rows


def _na_bias_kernel(rel_ref, o_ref):
    qc = lax.broadcasted_iota(jnp.int32, (GRID_W, GRID_W), 0)
    kc = lax.broadcasted_iota(jnp.int32, (GRID_W, GRID_W), 1)
    cs = jnp.clip(qc - NA_COLS // 2, 0, GRID_W - NA_COLS)
    col_ok = (kc >= cs) & (kc < cs + NA_COLS)
    dc = kc - qc + (NA_COLS - 1)
    hits = [dc == j for j in range(2 * NA_COLS - 1)]
    masked = jnp.full((GRID_W, GRID_W), MASK_VALUE, F32)
    tiles = []
    for dr in range(2 * NA_ROWS - 1):
        x = masked
        for j, hit in enumerate(hits):
            x = jnp.where(hit, rel_ref[0, 0, dr, j] * LOG2_E, x)
        tiles.append(jnp.where(col_ok, x, MASK_VALUE))
    kinds, rows = _na_block_geometry()
    for kind, (r0, kb) in enumerate(kinds):
        for qr in range(NA_QROWS):
            r = r0 + qr
            rs = min(max(r - NA_ROWS // 2, 0), rows - NA_ROWS)
            for kr in range(NA_KROWS):
                krow = kb + kr
                tile = tiles[krow - r + NA_ROWS - 1] if rs <= krow < rs + NA_ROWS else masked
                o_ref[0, kind, 0, qr * GRID_W:(qr + 1) * GRID_W,
                      kr * GRID_W:(kr + 1) * GRID_W] = tile


def _na_bias(na_bias):
    depth = na_bias.shape[0]
    return pl.pallas_call(
        _na_bias_kernel,
        grid=(depth, N_HEADS),
        in_specs=[pl.BlockSpec((1, 1) + na_bias.shape[2:], lambda l, h: (l, h, 0, 0),
                               memory_space=pltpu.SMEM)],
        out_specs=pl.BlockSpec((1, 3, 1, NA_QBLK, NA_KBLK), lambda l, h: (l, 0, h, 0, 0)),
        out_shape=jax.ShapeDtypeStruct((depth, 3, N_HEADS, NA_QBLK, NA_KBLK), F32),
        compiler_params=_params("arbitrary", "arbitrary"),
        name="na_bias",
    )(na_bias)


def _na_attn_kernel(q_ref, k_ref, v_ref, kc_ref, vc_ref, bias_ref, o_ref, *, n_blocks):
    blk = pl.program_id(1)
    start = pl.multiple_of(jnp.clip(blk - 1, 0, n_blocks - 3) * NA_QBLK, NA_QBLK)
    for h in range(N_HEADS):
        sl = slice(h * HEAD_DIM, (h + 1) * HEAD_DIM)
        q = q_ref[:, sl]
        kw = k_ref[pl.ds(start, NA_KBLK), sl]
        vw = v_ref[pl.ds(start, NA_KBLK), sl]
        s_loc = _dot_nt(q, kw) * LOGIT_SCALE + bias_ref[0, 0, h]
        s_ctx = _dot_nt(q, kc_ref[0, 0, :, sl].astype(BF16)) * LOGIT_SCALE
        mx = jnp.maximum(jnp.max(s_loc, axis=-1, keepdims=True),
                         jnp.max(s_ctx, axis=-1, keepdims=True))
        e_loc = jnp.exp2(s_loc - mx)
        e_ctx = jnp.exp2(s_ctx - mx)
        den = jnp.sum(e_loc, axis=-1, keepdims=True) + jnp.sum(e_ctx, axis=-1, keepdims=True)
        o = _dot(e_loc.astype(BF16), vw) + _dot(e_ctx.astype(BF16), vc_ref[0, 0, :, sl].astype(BF16))
        o_ref[:, sl] = (o / den).astype(BF16)


def _neighbourhood_attention(qn, kn, proj, k_ctx, v_ctx, bias, layer, *, seq):
    m = qn.shape[0]
    batch = m // seq
    n_blocks = seq // NA_QBLK
    past = k_ctx.shape[2]
    kind = lambda i: jnp.where(i == 0, 0, jnp.where(i == n_blocks - 1, 2, 1))
    ctx = pl.BlockSpec((1, 1, past, D_ATT), lambda b, i: (b, layer, 0, 0))
    return pl.pallas_call(
        functools.partial(_na_attn_kernel, n_blocks=n_blocks),
        grid=(batch, n_blocks),
        in_specs=[
            pl.BlockSpec((NA_QBLK, D_ATT), lambda b, i: (b * n_blocks + i, 0)),
            pl.BlockSpec((seq, D_ATT), lambda b, i: (b, 0)),
            pl.BlockSpec((seq, D_ATT), lambda b, i: (b, 2)),
            ctx, ctx,
            pl.BlockSpec((1, 1, N_HEADS, NA_QBLK, NA_KBLK), lambda b, i: (layer, kind(i), 0, 0, 0)),
        ],
        out_specs=pl.BlockSpec((NA_QBLK, D_ATT), lambda b, i: (b * n_blocks + i, 0)),
        out_shape=jax.ShapeDtypeStruct((m, D_ATT), BF16),
        compiler_params=_params("arbitrary", "arbitrary"),
        name="neighbourhood_attention",
    )(qn, kn, proj, k_ctx, v_ctx, bias)


def _stage_padded(pad_ref, x):
    zeros = jnp.zeros((PAD, pad_ref.shape[1]), F32)
    rows = x.shape[0]
    pad_ref[0:PAD, :] = zeros
    pad_ref[PAD + rows:2 * PAD + rows, :] = zeros
    pad_ref[PAD:PAD + rows, :] = x


def _conv_taps(pad_ref, seg, w, left, isolated):
    base = seg * SEG_LEN
    t = lax.broadcasted_iota(jnp.int32, (SEG_LEN, 1), 0)
    acc = None
    for k in range(w.shape[0]):
        off = k - left
        xs = pad_ref[PAD + base + off:PAD + base + off + SEG_LEN, :]
        if isolated and off != 0:
            xs = jnp.where((t + off >= 0) & (t + off < SEG_LEN), xs, 0.0)
        term = xs * w[k:k + 1, :]
        acc = term if acc is None else acc + term
    return acc


def _gelu_tanh(x):
    return x * (0.5 * (1.0 + jnp.tanh(0.7978845608028654 * (x + 0.044715 * (x * x * x)))))


def _lru_kernel(lxy_ref, h0_ref, cw_ref, cb_ref, w_ref, b_ref, lam_ref, *rest, chained):
    if chained:
        y_ref, xpad, a_tm, u_tm, p_tm = rest
    else:
        y_ref, fin_ref, xpad, a_tm, u_tm = rest
    last = (SEG_LEN - 1) * SEG_PER_BLOCK
    _stage_padded(xpad, lxy_ref[:, 0:D_LRU].astype(F32))
    cw = cw_ref[0]
    neg_lam = -lam_ref[0]
    softplus = jnp.maximum(neg_lam, 0.0) + jnp.log1p(jnp.exp(-jnp.abs(neg_lam)))
    decay = (-0.5 * LRU_C) * softplus
    for c in range(SEG_PER_BLOCK):
        xc = cb_ref[0] + _conv_taps(xpad, c, cw, LRU_CONV // 2, not chained)
        seg_rows = pl.ds(c, SEG_LEN, stride=SEG_PER_BLOCK)
        for n in range(LRU_BLOCKS):
            cols = slice(n * LRU_BW, (n + 1) * LRU_BW)
            xb = xc[:, cols]
            g = _dot(xb.astype(BF16), w_ref[0, n].astype(BF16)) + b_ref[0, n:n + 1, :]
            x_half = 0.5 * xb
            for d in range(2):
                ta = jnp.tanh(g[:, (2 * d) * LRU_BW:(2 * d + 1) * LRU_BW])
                ti = jnp.tanh(g[:, (2 * d + 1) * LRU_BW:(2 * d + 2) * LRU_BW])
                log_a = decay[d:d + 1, cols] * (1.0 + ta)
                th = jnp.tanh(log_a)
                p = -2.0 * th
                coef = p * lax.rsqrt(jnp.maximum(p * (1.0 - th), F32_MIN_NORMAL))
                a_tm[d, n, seg_rows, :] = jnp.exp(log_a)
                u_tm[d, n, seg_rows, :] = coef * ((1.0 + ti) * x_half)

    def scan_dir(d, rows, h, p):
        h_new, p_new = [], []
        for n in range(LRU_BLOCKS):
            a = a_tm[d, n, rows, :]
            hn = a * h[n] + u_tm[d, n, rows, :]
            u_tm[d, n, rows, :] = hn
            h_new.append(hn)
            if chained:
                pn = a * p[n]
                p_tm[d, n, rows, :] = pn
                p_new.append(pn)
        return tuple(h_new), tuple(p_new)

    def step(t, carry):
        hf, pf, hb, pb = carry
        hf, pf = scan_dir(0, pl.ds(pl.multiple_of(t * SEG_PER_BLOCK, SEG_PER_BLOCK), SEG_PER_BLOCK), hf, pf)
        tb = SEG_LEN - 1 - t
        hb, pb = scan_dir(1, pl.ds(pl.multiple_of(tb * SEG_PER_BLOCK, SEG_PER_BLOCK), SEG_PER_BLOCK), hb, pb)
        return hf, pf, hb, pb

    lane_blocks = [slice(n * LRU_BW, (n + 1) * LRU_BW) for n in range(LRU_BLOCKS)]
    if chained:
        zero = tuple(jnp.zeros((SEG_PER_BLOCK, LRU_BW), F32) for _ in lane_blocks)
        one = tuple(jnp.ones((SEG_PER_BLOCK, LRU_BW), F32) for _ in lane_blocks)
        init = (zero, one, zero, one)
    else:
        init = (tuple(h0_ref[0, :, cols] for cols in lane_blocks), (),
                tuple(h0_ref[1, :, cols] for cols in lane_blocks), ())
    lax.fori_loop(0, SEG_LEN, step, init, unroll=4)

    carry_f = carry_b = None
    if chained:
        carry_f, carry_b = [], []
        for n, cols in enumerate(lane_blocks):
            end_h, end_p = u_tm[0, n, last:last + SEG_PER_BLOCK, :], p_tm[0, n, last:last + SEG_PER_BLOCK, :]
            cf = [h0_ref[0, 0:1, cols]]
            for s in range(SEG_PER_BLOCK - 1):
                cf.append(end_h[s:s + 1] + end_p[s:s + 1] * cf[s])
            carry_f.append(cf)
            beg_h, beg_p = u_tm[1, n, 0:SEG_PER_BLOCK, :], p_tm[1, n, 0:SEG_PER_BLOCK, :]
            cb = [None] * SEG_PER_BLOCK
            cb[SEG_PER_BLOCK - 1] = h0_ref[0, 1:2, cols]
            for s in range(SEG_PER_BLOCK - 1, 0, -1):
                cb[s - 1] = beg_h[s:s + 1] + beg_p[s:s + 1] * cb[s]
            carry_b.append(cb)
    else:
        for n, cols in enumerate(lane_blocks):
            fin_ref[0, :, cols] = u_tm[0, n, last:last + SEG_PER_BLOCK, :]
            fin_ref[1, :, cols] = u_tm[1, n, 0:SEG_PER_BLOCK, :]

    for c in range(SEG_PER_BLOCK):
        seg_rows = pl.ds(c, SEG_LEN, stride=SEG_PER_BLOCK)
        tok = slice(c * SEG_LEN, (c + 1) * SEG_LEN)
        for n, cols in enumerate(lane_blocks):
            h = u_tm[0, n, seg_rows, :] + u_tm[1, n, seg_rows, :]
            if chained:
                h = h + p_tm[0, n, seg_rows, :] * carry_f[n][c] + p_tm[1, n, seg_rows, :] * carry_b[n][c]
            ly = lxy_ref[tok, D_LRU + n * LRU_BW:D_LRU + (n + 1) * LRU_BW].astype(F32)
            y_ref[tok, cols] = (h * _gelu_tanh(ly)).astype(BF16)


def _rglru(proj, h0, conv_w, conv_b, w_cat, b_cat, lam, layer, *, chained):
    m = proj.shape[0]
    blocks = m // MIX_ROWS
    lxy_col = 3 * D_ATT // (2 * D_LRU)
    per_layer = lambda *shape: pl.BlockSpec((1,) + shape, lambda b: (layer,) + (0,) * len(shape))
    tm_buf = pltpu.VMEM((2, LRU_BLOCKS, MIX_ROWS, LRU_BW), F32)
    if chained:
        h0_spec = pl.BlockSpec((1, 2, D_LRU), lambda b: (b, 0, 0))
        out_specs = [pl.BlockSpec((MIX_ROWS, D_LRU), lambda b: (b, 0))]
        out_shape = [jax.ShapeDtypeStruct((m, D_LRU), BF16)]
        scratch = [tm_buf, tm_buf, tm_buf]
    else:
        h0_spec = pl.BlockSpec((2, SEG_PER_BLOCK, D_LRU), lambda b: (0, b, 0))
        out_specs = [pl.BlockSpec((MIX_ROWS, D_LRU), lambda b: (b, 0)),
                     pl.BlockSpec((2, SEG_PER_BLOCK, D_LRU), lambda b: (0, b, 0))]
        out_shape = [jax.ShapeDtypeStruct((m, D_LRU), BF16),
                     jax.ShapeDtypeStruct((2, blocks * SEG_PER_BLOCK, D_LRU), F32)]
        scratch = [tm_buf, tm_buf]
    return pl.pallas_call(
        functools.partial(_lru_kernel, chained=chained),
        grid=(blocks,),
        in_specs=[
            pl.BlockSpec((MIX_ROWS, 2 * D_LRU), lambda b: (b, lxy_col)),
            h0_spec,
            per_layer(LRU_CONV, D_LRU), per_layer(1, D_LRU),
            per_layer(LRU_BLOCKS, LRU_BW, 4 * LRU_BW), per_layer(LRU_BLOCKS, 4 * LRU_BW),
            per_layer(2, D_LRU),
        ],
        out_specs=out_specs,
        out_shape=out_shape,
        scratch_shapes=[pltpu.VMEM((MIX_ROWS + 2 * PAD, D_LRU), F32)] + scratch,
        compiler_params=_params("arbitrary"),
        name="rglru",
    )(proj, h0, conv_w, conv_b.reshape(conv_b.shape[0], 1, D_LRU), w_cat, b_cat, lam)


def _sconv_kernel(bc_ref, x_ref, w_ref, y_ref, pad, *, isolated):
    _stage_padded(pad, bc_ref[:, D_SC:2 * D_SC].astype(F32) * x_ref[...].astype(F32))
    w = w_ref[0]
    for c in range(SEG_PER_BLOCK):
        tok = slice(c * SEG_LEN, (c + 1) * SEG_LEN)
        conv = _conv_taps(pad, c, w, SC_CONV // 2, isolated)
        y_ref[tok, :] = (bc_ref[tok, 0:D_SC].astype(F32) * conv).astype(BF16)


def _short_conv(proj, w, layer, *, isolated):
    m = proj.shape[0]
    col0 = 3 * D_ATT + 2 * D_LRU
    return pl.pallas_call(
        functools.partial(_sconv_kernel, isolated=isolated),
        grid=(m // MIX_ROWS,),
        in_specs=[pl.BlockSpec((MIX_ROWS, 2 * D_SC), lambda b: (b, col0 // (2 * D_SC))),
                  pl.BlockSpec((MIX_ROWS, D_SC), lambda b: (b, col0 // D_SC + 2)),
                  pl.BlockSpec((1, SC_CONV, D_SC), lambda b: (layer, 0, 0))],
        out_specs=pl.BlockSpec((MIX_ROWS, D_SC), lambda b: (b, 0)),
        out_shape=jax.ShapeDtypeStruct((m, D_SC), BF16),
        scratch_shapes=[pltpu.VMEM((MIX_ROWS + 2 * PAD, D_SC), F32)],
        compiler_params=_params("arbitrary"),
        name="short_conv",
    )(proj, proj, w)


def _lru_gate_weights(w_a, b_a, w_i, b_i):
    depth = w_a.shape[0]
    w_cat = jnp.concatenate([w_a[:, 0], w_i[:, 0], w_a[:, 1], w_i[:, 1]], axis=-1)
    blocks = lambda b: b.reshape(depth, 2, LRU_BLOCKS, LRU_BW)
    ba, bi = blocks(b_a), blocks(b_i)
    b_cat = jnp.concatenate([ba[:, 0], bi[:, 0], ba[:, 1], bi[:, 1]], axis=-1)
    return 0.5 * w_cat, 0.5 * b_cat


def _stream_layer(h, mod, p, layer, *, rows_per_mod, ctx, caches):
    m = h.shape[0]
    proj = _norm_matmul(h, mod, p["g_mix"], [p["w_in"]], layer, shift_idx=0, scale_idx=1,
                        rows_per_mod=rows_per_mod, tn=512, name="in_proj")
    if ctx is None:
        k_acc, v_acc = caches
        att, k_acc, v_acc = _context_attention(proj, p["q_gain"], p["k_gain"], layer, k_acc, v_acc,
                                               seq=SEG_LEN)
        caches = (k_acc, v_acc)
        h0 = jnp.zeros((2, m // SEG_LEN, D_LRU), F32)
        lru, lru_final = _rglru(proj, h0, p["lru_conv_w"], p["lru_conv_b"], p["w_cat"], p["b_cat"],
                                p["lru_lambda"], layer, chained=False)
    else:
        k_ctx, v_ctx, h0, bias = ctx
        qn, kn = _qk_norm(proj, p["q_gain"], p["k_gain"], layer)
        att = _neighbourhood_attention(qn, kn, proj, k_ctx, v_ctx, bias, layer, seq=MIX_ROWS)
        (lru,) = _rglru(proj, h0, p["lru_conv_w"], p["lru_conv_b"], p["w_cat"], p["b_cat"],
                        p["lru_lambda"], layer, chained=True)
        lru_final = None
    conv = _short_conv(proj, p["sc_conv_w"], layer, isolated=ctx is None)
    h = _matmul_residual([att, lru, conv], p["w_out"], layer, h, mod, gate_idx=2,
                         rows_per_mod=rows_per_mod, tm=m, tn=256, ca=512, name="out_proj")
    hid = _norm_matmul(h, mod, p["g_ffn"], [p["w_ffn_gate"], p["w_ffn_up"]], layer, shift_idx=3,
                       scale_idx=4, rows_per_mod=rows_per_mod, tn=256, name="ffn_up")
    h = _matmul_residual([hid], p["w_ffn_down"], layer, h, mod, gate_idx=5,
                         rows_per_mod=rows_per_mod, tm=m // 2, tn=256, ca=256, name="ffn_down")
    return h, caches, lru_final


def kernel(x_prompt, x_sample, cache_k, cache_v, state_lru, c, c_ctx, w_mod, b_mod, g_mix, g_ffn, w_in, q_gain, k_gain, na_bias, lru_conv_w, lru_conv_b, lru_w_a, lru_b_a, lru_w_i, lru_b_i, lru_lambda, sc_conv_w, w_out, w_ffn_gate, w_ffn_up, w_ffn_down):
    batch, seq, d = x_prompt.shape
    dec_batch, dec_seq, _ = x_sample.shape
    depth = w_mod.shape[0]
    past = cache_k.shape[2]
    assert seq == SEG_LEN and dec_seq == MIX_ROWS and batch % SEG_PER_BLOCK == 0

    cond = jnp.zeros((MOD_ROWS, d), F32).at[0].set(c_ctx).at[1:1 + dec_batch].set(c)
    mod_all = _modulation(cond, w_mod, b_mod).reshape(depth, MOD_ROWS, 6, d)
    bias_all = _na_bias(na_bias)
    w_cat, b_cat = _lru_gate_weights(lru_w_a, lru_b_a, lru_w_i, lru_b_i)
    p = {
        "g_mix": g_mix, "g_ffn": g_ffn, "w_in": w_in, "q_gain": q_gain, "k_gain": k_gain,
        "lru_conv_w": lru_conv_w, "lru_conv_b": lru_conv_b, "w_cat": w_cat, "b_cat": b_cat,
        "lru_lambda": lru_lambda, "sc_conv_w": sc_conv_w, "w_out": w_out,
        "w_ffn_gate": w_ffn_gate, "w_ffn_up": w_ffn_up, "w_ffn_down": w_ffn_down,
    }

    k_ctx = cache_k.reshape(dec_batch, depth, past, D_ATT)
    v_ctx = cache_v.reshape(dec_batch, depth, past, D_ATT)

    hp = x_prompt.reshape(batch * seq, d)
    hs = x_sample.reshape(dec_batch * dec_seq, d)
    caches = (jnp.zeros((batch, depth, seq, D_ATT), F32), jnp.zeros((batch, depth, seq, D_ATT), F32))
    new_s = []
    for l in range(depth):
        hp, caches, s_l = _stream_layer(hp, mod_all[l, 0:1], p, l, rows_per_mod=batch * seq,
                                        ctx=None, caches=caches)
        new_s.append(jnp.swapaxes(s_l, 0, 1))
        ctx = (k_ctx, v_ctx, state_lru[:, l], bias_all)
        hs, _, _ = _stream_layer(hs, mod_all[l, 1:1 + dec_batch], p, l, rows_per_mod=dec_seq,
                                 ctx=ctx, caches=None)
    cache_shape = (batch, depth, seq, N_HEADS, HEAD_DIM)
    return (hp.reshape(batch, seq, d), hs.reshape(dec_batch, dec_seq, d),
            caches[0].reshape(cache_shape), caches[1].reshape(cache_shape), jnp.stack(new_s, axis=1))
```

```python
import functools

import jax
import jax.numpy as jnp
from jax import lax
from jax.experimental import pallas as pl
from jax.experimental.pallas import tpu as pltpu

D_MODEL = 2048
N_HEADS = 8
HEAD_DIM = 128
D_ATT = N_HEADS * HEAD_DIM
D_LRU = 512
LRU_BLOCKS = 4
LRU_BW = 128
LRU_CONV = 4
LRU_C = 8.0
D_SC = 512
SC_CONV = 3
GRID_W = 64
NA_ROWS = 8
NA_COLS = 16
EPS = 1e-6
LOG2_E = 1.4426950408889634
LOGIT_SCALE = HEAD_DIM ** -0.5 * LOG2_E

NA_QROWS = 4
NA_KROWS = 12
NA_QBLK = NA_QROWS * GRID_W
NA_KBLK = NA_KROWS * GRID_W
MASK_VALUE = -1e30

LANES = 128
SUBLANES = 8
MOD_ROWS = SUBLANES
VMEM_LIMIT_BYTES = 60 * 1024 * 1024

SEG_LEN = 256
SEG_PER_BLOCK = SUBLANES
MIX_ROWS = SEG_LEN * SEG_PER_BLOCK
PAD = SUBLANES
MM_ROWS = 1024

F32 = jnp.float32
BF16 = jnp.bfloat16
F32_MIN_NORMAL = 1.1754943508222875e-38


def _params(*semantics):
    return pltpu.CompilerParams(dimension_semantics=semantics,
                                vmem_limit_bytes=VMEM_LIMIT_BYTES)


def _dot(a, b):
    return jnp.dot(a, b, preferred_element_type=F32)


def _dot_nt(a, b):
    return lax.dot_general(a, b, (((1,), (1,)), ((), ())), preferred_element_type=F32)


def _sigmoid(x):
    return 0.5 * (1.0 + jnp.tanh(0.5 * x))


def _mod_kernel(c_ref, w_ref, b_ref, o_ref):
    c = c_ref[...]
    s = (c * _sigmoid(c)).astype(BF16)
    o_ref[0] = _dot(s, w_ref[0].astype(BF16)) + b_ref[0]


def _modulation(cond, w_mod, b_mod, *, tn=1024):
    depth, d, n = w_mod.shape
    return pl.pallas_call(
        _mod_kernel,
        grid=(depth, n // tn),
        in_specs=[
            pl.BlockSpec((MOD_ROWS, d), lambda l, j: (0, 0)),
            pl.BlockSpec((1, d, tn), lambda l, j: (l, 0, j)),
            pl.BlockSpec((1, 1, tn), lambda l, j: (l, 0, j)),
        ],
        out_specs=pl.BlockSpec((1, MOD_ROWS, tn), lambda l, j: (l, 0, j)),
        out_shape=jax.ShapeDtypeStruct((depth, MOD_ROWS, n), F32),
        compiler_params=_params("arbitrary", "arbitrary"),
        name="modulation",
    )(cond, w_mod, b_mod.reshape(depth, 1, n))


def _norm_mm_kernel(x_ref, mod_ref, g_ref, *rest, n_w, shift_idx, scale_idx, n_chunks):
    w_refs, o_ref, u_ref, wb_refs = rest[:n_w], rest[n_w], rest[n_w + 1], rest[n_w + 2:]
    s = pl.program_id(0)
    cm = x_ref.shape[0]
    slot = jnp.maximum(s - n_chunks, 0) % 2

    def cast_weights(into):
        for w_ref, wb_ref in zip(w_refs, wb_refs):
            wb_ref[into] = w_ref[0].astype(BF16)

    def matmul_rows(row0, rows, use):
        sl = pl.ds(pl.multiple_of(row0, rows), rows)
        u = u_ref[sl, :]
        a = _dot(u, wb_refs[0][use])
        if n_w == 2:
            a = a * _sigmoid(a) * _dot(u, wb_refs[1][use])
        o_ref[sl, :] = a.astype(o_ref.dtype)

    def norm_chunk():
        gain = g_ref[0] * (1.0 + mod_ref[0, scale_idx:scale_idx + 1, :])
        shift = mod_ref[0, shift_idx:shift_idx + 1, :]
        for p in range(cm // LANES):
            x = x_ref[p * LANES:(p + 1) * LANES, :]
            y = x * lax.rsqrt(jnp.mean(x * x, axis=-1, keepdims=True) + EPS)
            u_ref[pl.ds(pl.multiple_of(s * cm + p * LANES, LANES), LANES), :] = (y * gain + shift).astype(BF16)

    @pl.when(s == 0)
    def _():
        cast_weights(0)
        norm_chunk()

    @pl.when((s >= 1) & (s < n_chunks))
    def _():
        matmul_rows((s - 1) * cm, cm, 0)
        norm_chunk()

    @pl.when(s == n_chunks)
    def _():
        matmul_rows((n_chunks - 1) * cm, cm, 0)
        cast_weights(1)

    @pl.when(s > n_chunks)
    def _():
        for r in range(u_ref.shape[0] // MM_ROWS):
            matmul_rows(r * MM_ROWS, MM_ROWS, slot)
        cast_weights(1 - slot)


def _norm_matmul(x, mod, gain, weights, layer, *, shift_idx, scale_idx, rows_per_mod, tn, name,
                 cm=512):
    m, d = x.shape
    n = weights[0].shape[2]
    n_chunks = m // cm
    n_tiles = n // tn
    chunks_per_mod = rows_per_mod // cm
    chunk = lambda s: jnp.minimum(s, n_chunks - 1)
    tile = lambda s: jnp.maximum(s - n_chunks, 0)
    next_tile = lambda s: jnp.minimum(tile(s + 1), n_tiles - 1)
    kernel = functools.partial(_norm_mm_kernel, n_w=len(weights), shift_idx=shift_idx,
                               scale_idx=scale_idx, n_chunks=n_chunks)
    return pl.pallas_call(
        kernel,
        grid=(n_chunks + n_tiles,),
        in_specs=[
            pl.BlockSpec((cm, d), lambda s: (chunk(s), 0)),
            pl.BlockSpec((1, 6, d), lambda s: (chunk(s) // chunks_per_mod, 0, 0)),
            pl.BlockSpec((1, 1, d), lambda s: (layer, 0, 0)),
        ] + [pl.BlockSpec((1, d, tn), lambda s: (layer, 0, next_tile(s))) for _ in weights],
        out_specs=pl.BlockSpec((m, tn), lambda s: (0, tile(s))),
        out_shape=jax.ShapeDtypeStruct((m, n), BF16),
        scratch_shapes=[pltpu.VMEM((m, d), BF16)] + [pltpu.VMEM((2, d, tn), BF16) for _ in weights],
        compiler_params=_params("arbitrary"),
        name=name,
    )(x, mod, gain.reshape(gain.shape[0], 1, d), *weights)


def _mm_res_kernel(*refs, splits, gate_idx, rows_per_mod, n_chunks):
    n = len(splits)
    a_refs = refs[:n]
    w_ref, h_ref, mod_ref, o_ref, a_res, wb_ref = refs[n:]
    s = pl.program_id(1)
    tm = h_ref.shape[0]
    ca = a_refs[0].shape[0]
    tile_row0 = pl.program_id(0) * tm
    slot = jnp.maximum(s - n_chunks + 1, 0) % 2

    def cast_weights(into):
        wb_ref[into] = w_ref[0].astype(BF16)

    def product(row0, rows, use):
        sl = pl.ds(pl.multiple_of(row0, rows), rows)
        gate = mod_ref[(tile_row0 + row0) // rows_per_mod, gate_idx:gate_idx + 1, :]
        o_ref[sl, :] = h_ref[sl, :] + gate * _dot(a_res[sl, :], wb_ref[use])

    def stage_chunk():
        rows = pl.ds(pl.multiple_of(s * ca, ca), ca)
        k0 = 0
        for a_ref, kk in zip(a_refs, splits):
            a_res[rows, k0:k0 + kk] = a_ref[...]
            k0 += kk
        product(s * ca, ca, 0)

    @pl.when(s == 0)
    def _():
        cast_weights(0)
        stage_chunk()

    @pl.when((s >= 1) & (s < n_chunks - 1))
    def _():
        stage_chunk()

    @pl.when(s == n_chunks - 1)
    def _():
        stage_chunk()
        cast_weights(1)

    @pl.when(s >= n_chunks)
    def _():
        for r in range(tm // MM_ROWS):
            product(r * MM_ROWS, MM_ROWS, slot)
        cast_weights(1 - slot)


def _matmul_residual(a_parts, w, layer, h, mod, *, gate_idx, rows_per_mod, tm, tn, ca, name):
    m, n = h.shape
    k = w.shape[1]
    splits = tuple(a.shape[1] for a in a_parts)
    n_chunks = tm // ca
    n_tiles = n // tn
    assert n_chunks >= 2
    chunk = lambda i, s: i * n_chunks + jnp.minimum(s, n_chunks - 1)
    tile = lambda s: jnp.maximum(s - n_chunks + 1, 0)
    next_tile = lambda s: jnp.minimum(tile(s + 1), n_tiles - 1)
    kernel = functools.partial(_mm_res_kernel, splits=splits, gate_idx=gate_idx,
                               rows_per_mod=rows_per_mod, n_chunks=n_chunks)
    return pl.pallas_call(
        kernel,
        grid=(m // tm, n_chunks + n_tiles - 1),
        in_specs=[pl.BlockSpec((ca, kk), lambda i, s: (chunk(i, s), 0)) for kk in splits] + [
            pl.BlockSpec((1, k, tn), lambda i, s: (layer, 0, next_tile(s))),
            pl.BlockSpec((tm, tn), lambda i, s: (i, tile(s))),
            pl.BlockSpec((mod.shape[0], 6, tn), lambda i, s: (0, 0, tile(s))),
        ],
        out_specs=pl.BlockSpec((tm, tn), lambda i, s: (i, tile(s))),
        out_shape=jax.ShapeDtypeStruct((m, n), F32),
        scratch_shapes=[pltpu.VMEM((tm, k), BF16), pltpu.VMEM((2, k, tn), BF16)],
        compiler_params=_params("arbitrary", "arbitrary"),
        name=name,
    )(*a_parts, w, h, mod)


def _head_rms(x, gain):
    return x * lax.rsqrt(jnp.mean(x * x, axis=-1, keepdims=True) + EPS) * gain


def _ctx_attn_kernel(qkv_ref, qg_ref, kg_ref, kacc_ref, vacc_ref, att_ref, kout_ref, vout_ref):
    del kacc_ref, vacc_ref
    qg = qg_ref[0]
    kg = kg_ref[0]
    n_seq, _, seq, _ = kout_ref.shape
    for j in range(n_seq):
        rows = slice(j * seq, (j + 1) * seq)
        for h in range(N_HEADS):
            sl = slice(h * HEAD_DIM, (h + 1) * HEAD_DIM)
            qn = _head_rms(qkv_ref[rows, sl].astype(F32), qg)
            kn = _head_rms(qkv_ref[rows, D_ATT + h * HEAD_DIM:D_ATT + (h + 1) * HEAD_DIM].astype(F32), kg)
            v = qkv_ref[rows, 2 * D_ATT + h * HEAD_DIM:2 * D_ATT + (h + 1) * HEAD_DIM]
            kout_ref[j, 0, :, sl] = kn
            vout_ref[j, 0, :, sl] = v.astype(F32)
            s = _dot_nt(qn.astype(BF16), kn.astype(BF16)) * LOGIT_SCALE
            e = jnp.exp2(s - jnp.max(s, axis=-1, keepdims=True))
            den = jnp.sum(e, axis=-1, keepdims=True)
            att_ref[rows, sl] = (_dot(e.astype(BF16), v) / den).astype(BF16)


def _context_attention(proj, q_gain, k_gain, layer, k_acc, v_acc, *, seq, n_seq=2):
    m = proj.shape[0]
    gain = pl.BlockSpec((1, 1, HEAD_DIM), lambda b: (layer, 0, 0))
    cache = pl.BlockSpec((n_seq, 1, seq, D_ATT), lambda b: (b, layer, 0, 0))
    hbm = pl.BlockSpec(memory_space=pl.ANY)
    gains = lambda g: g.reshape(g.shape[0], 1, HEAD_DIM)
    return pl.pallas_call(
        _ctx_attn_kernel,
        grid=(m // (n_seq * seq),),
        in_specs=[pl.BlockSpec((n_seq * seq, 3 * D_ATT), lambda b: (b, 0)), gain, gain, hbm, hbm],
        out_specs=[pl.BlockSpec((n_seq * seq, D_ATT), lambda b: (b, 0)), cache, cache],
        out_shape=[jax.ShapeDtypeStruct((m, D_ATT), BF16),
                   jax.ShapeDtypeStruct(k_acc.shape, F32),
                   jax.ShapeDtypeStruct(v_acc.shape, F32)],
        input_output_aliases={3: 1, 4: 2},
        compiler_params=_params("arbitrary"),
        name="context_attention",
    )(proj, gains(q_gain), gains(k_gain), k_acc, v_acc)


def _qk_norm_kernel(qk_ref, qg_ref, kg_ref, qo_ref, ko_ref):
    qg = qg_ref[0]
    kg = kg_ref[0]
    for h in range(N_HEADS):
        sl = slice(h * HEAD_DIM, (h + 1) * HEAD_DIM)
        qo_ref[:, sl] = _head_rms(qk_ref[:, sl].astype(F32), qg).astype(BF16)
        k = qk_ref[:, D_ATT + h * HEAD_DIM:D_ATT + (h + 1) * HEAD_DIM]
        ko_ref[:, sl] = _head_rms(k.astype(F32), kg).astype(BF16)


def _qk_norm(proj, q_gain, k_gain, layer, *, tm=1024):
    m = proj.shape[0]
    gain = pl.BlockSpec((1, 1, HEAD_DIM), lambda i: (layer, 0, 0))
    out = pl.BlockSpec((tm, D_ATT), lambda i: (i, 0))
    gains = lambda g: g.reshape(g.shape[0], 1, HEAD_DIM)
    return pl.pallas_call(
        _qk_norm_kernel,
        grid=(m // tm,),
        in_specs=[pl.BlockSpec((tm, 2 * D_ATT), lambda i: (i, 0)), gain, gain],
        out_specs=[out, out],
        out_shape=[jax.ShapeDtypeStruct((m, D_ATT), BF16)] * 2,
        compiler_params=_params("arbitrary"),
        name="qk_norm",
    )(proj, gains(q_gain), gains(k_gain))


def _na_block_geometry():
    rows = 32
    return ((0, 0), (NA_QROWS, 0), (rows - NA_QROWS, rows - NA_KROWS)), rows


def _na_bias_kernel(rel_ref, o_ref):
    qc = lax.broadcasted_iota(jnp.int32, (GRID_W, GRID_W), 0)
    kc = lax.broadcasted_iota(jnp.int32, (GRID_W, GRID_W), 1)
    cs = jnp.clip(qc - NA_COLS // 2, 0, GRID_W - NA_COLS)
    col_ok = (kc >= cs) & (kc < cs + NA_COLS)
    dc = kc - qc + (NA_COLS - 1)
    hits = [dc == j for j in range(2 * NA_COLS - 1)]
    masked = jnp.full((GRID_W, GRID_W), MASK_VALUE, F32)
    tiles = []
    for dr in range(2 * NA_ROWS - 1):
        x = masked
        for j, hit in enumerate(hits):
            x = jnp.where(hit, rel_ref[0, 0, dr, j] * LOG2_E, x)
        tiles.append(jnp.where(col_ok, x, MASK_VALUE))
    kinds, rows = _na_block_geometry()
    for kind, (r0, kb) in enumerate(kinds):
        for qr in range(NA_QROWS):
            r = r0 + qr
            rs = min(max(r - NA_ROWS // 2, 0), rows - NA_ROWS)
            for kr in range(NA_KROWS):
                krow = kb + kr
                tile = tiles[krow - r + NA_ROWS - 1] if rs <= krow < rs + NA_ROWS else masked
                o_ref[0, kind, 0, qr * GRID_W:(qr + 1) * GRID_W,
                      kr * GRID_W:(kr + 1) * GRID_W] = tile


def _na_bias(na_bias):
    depth = na_bias.shape[0]
    return pl.pallas_call(
        _na_bias_kernel,
        grid=(depth, N_HEADS),
        in_specs=[pl.BlockSpec((1, 1) + na_bias.shape[2:], lambda l, h: (l, h, 0, 0),
                               memory_space=pltpu.SMEM)],
        out_specs=pl.BlockSpec((1, 3, 1, NA_QBLK, NA_KBLK), lambda l, h: (l, 0, h, 0, 0)),
        out_shape=jax.ShapeDtypeStruct((depth, 3, N_HEADS, NA_QBLK, NA_KBLK), F32),
        compiler_params=_params("arbitrary", "arbitrary"),
        name="na_bias",
    )(na_bias)


def _na_attn_kernel(q_ref, k_ref, v_ref, kc_ref, vc_ref, bias_ref, o_ref, *, n_blocks):
    blk = pl.program_id(1)
    start = pl.multiple_of(jnp.clip(blk - 1, 0, n_blocks - 3) * NA_QBLK, NA_QBLK)
    for h in range(N_HEADS):
        sl = slice(h * HEAD_DIM, (h + 1) * HEAD_DIM)
        q = q_ref[:, sl]
        kw = k_ref[pl.ds(start, NA_KBLK), sl]
        vw = v_ref[pl.ds(start, NA_KBLK), sl]
        s_loc = _dot_nt(q, kw) * LOGIT_SCALE + bias_ref[0, 0, h]
        s_ctx = _dot_nt(q, kc_ref[0, 0, :, sl].astype(BF16)) * LOGIT_SCALE
        mx = jnp.maximum(jnp.max(s_loc, axis=-1, keepdims=True),
                         jnp.max(s_ctx, axis=-1, keepdims=True))
        e_loc = jnp.exp2(s_loc - mx)
        e_ctx = jnp.exp2(s_ctx - mx)
        den = jnp.sum(e_loc, axis=-1, keepdims=True) + jnp.sum(e_ctx, axis=-1, keepdims=True)
        o = _dot(e_loc.astype(BF16), vw) + _dot(e_ctx.astype(BF16), vc_ref[0, 0, :, sl].astype(BF16))
        o_ref[:, sl] = (o / den).astype(BF16)


def _neighbourhood_attention(qn, kn, proj, k_ctx, v_ctx, bias, layer, *, seq):
    m = qn.shape[0]
    batch = m // seq
    n_blocks = seq // NA_QBLK
    past = k_ctx.shape[2]
    kind = lambda i: jnp.where(i == 0, 0, jnp.where(i == n_blocks - 1, 2, 1))
    ctx = pl.BlockSpec((1, 1, past, D_ATT), lambda b, i: (b, layer, 0, 0))
    return pl.pallas_call(
        functools.partial(_na_attn_kernel, n_blocks=n_blocks),
        grid=(batch, n_blocks),
        in_specs=[
            pl.BlockSpec((NA_QBLK, D_ATT), lambda b, i: (b * n_blocks + i, 0)),
            pl.BlockSpec((seq, D_ATT), lambda b, i: (b, 0)),
            pl.BlockSpec((seq, D_ATT), lambda b, i: (b, 2)),
            ctx, ctx,
            pl.BlockSpec((1, 1, N_HEADS, NA_QBLK, NA_KBLK), lambda b, i: (layer, kind(i), 0, 0, 0)),
        ],
        out_specs=pl.BlockSpec((NA_QBLK, D_ATT), lambda b, i: (b * n_blocks + i, 0)),
        out_shape=jax.ShapeDtypeStruct((m, D_ATT), BF16),
        compiler_params=_params("arbitrary", "arbitrary"),
        name="neighbourhood_attention",
    )(qn, kn, proj, k_ctx, v_ctx, bias)


def _stage_padded(pad_ref, x):
    zeros = jnp.zeros((PAD, pad_ref.shape[1]), F32)
    rows = x.shape[0]
    pad_ref[0:PAD, :] = zeros
    pad_ref[PAD + rows:2 * PAD + rows, :] = zeros
    pad_ref[PAD:PAD + rows, :] = x


def _conv_taps(pad_ref, seg, w, left, isolated):
    base = seg * SEG_LEN
    t = lax.broadcasted_iota(jnp.int32, (SEG_LEN, 1), 0)
    acc = None
    for k in range(w.shape[0]):
        off = k - left
        xs = pad_ref[PAD + base + off:PAD + base + off + SEG_LEN, :]
        if isolated and off != 0:
            xs = jnp.where((t + off >= 0) & (t + off < SEG_LEN), xs, 0.0)
        term = xs * w[k:k + 1, :]
        acc = term if acc is None else acc + term
    return acc


def _gelu_tanh(x):
    return x * (0.5 * (1.0 + jnp.tanh(0.7978845608028654 * (x + 0.044715 * (x * x * x)))))


def _lru_kernel(lxy_ref, h0_ref, cw_ref, cb_ref, w_ref, b_ref, lam_ref, *rest, chained):
    if chained:
        y_ref, xpad, a_tm, u_tm, p_tm = rest
    else:
        y_ref, fin_ref, xpad, a_tm, u_tm = rest
    last = (SEG_LEN - 1) * SEG_PER_BLOCK
    _stage_padded(xpad, lxy_ref[:, 0:D_LRU].astype(F32))
    cw = cw_ref[0]
    neg_lam = -lam_ref[0]
    softplus = jnp.maximum(neg_lam, 0.0) + jnp.log1p(jnp.exp(-jnp.abs(neg_lam)))
    decay = (-0.5 * LRU_C) * softplus
    for c in range(SEG_PER_BLOCK):
        xc = cb_ref[0] + _conv_taps(xpad, c, cw, LRU_CONV // 2, not chained)
        seg_rows = pl.ds(c, SEG_LEN, stride=SEG_PER_BLOCK)
        for n in range(LRU_BLOCKS):
            cols = slice(n * LRU_BW, (n + 1) * LRU_BW)
            xb = xc[:, cols]
            g = _dot(xb.astype(BF16), w_ref[0, n].astype(BF16)) + b_ref[0, n:n + 1, :]
            x_half = 0.5 * xb
            for d in range(2):
                ta = jnp.tanh(g[:, (2 * d) * LRU_BW:(2 * d + 1) * LRU_BW])
                ti = jnp.tanh(g[:, (2 * d + 1) * LRU_BW:(2 * d + 2) * LRU_BW])
                log_a = decay[d:d + 1, cols] * (1.0 + ta)
                th = jnp.tanh(log_a)
                p = -2.0 * th
                coef = p * lax.rsqrt(jnp.maximum(p * (1.0 - th), F32_MIN_NORMAL))
                a_tm[d, n, seg_rows, :] = jnp.exp(log_a)
                u_tm[d, n, seg_rows, :] = coef * ((1.0 + ti) * x_half)

    def scan_dir(d, rows, h, p):
        h_new, p_new = [], []
        for n in range(LRU_BLOCKS):
            a = a_tm[d, n, rows, :]
            hn = a * h[n] + u_tm[d, n, rows, :]
            u_tm[d, n, rows, :] = hn
            h_new.append(hn)
            if chained:
                pn = a * p[n]
                p_tm[d, n, rows, :] = pn
                p_new.append(pn)
        return tuple(h_new), tuple(p_new)

    def step(t, carry):
        hf, pf, hb, pb = carry
        hf, pf = scan_dir(0, pl.ds(pl.multiple_of(t * SEG_PER_BLOCK, SEG_PER_BLOCK), SEG_PER_BLOCK), hf, pf)
        tb = SEG_LEN - 1 - t
        hb, pb = scan_dir(1, pl.ds(pl.multiple_of(tb * SEG_PER_BLOCK, SEG_PER_BLOCK), SEG_PER_BLOCK), hb, pb)
        return hf, pf, hb, pb

    lane_blocks = [slice(n * LRU_BW, (n + 1) * LRU_BW) for n in range(LRU_BLOCKS)]
    if chained:
        zero = tuple(jnp.zeros((SEG_PER_BLOCK, LRU_BW), F32) for _ in lane_blocks)
        one = tuple(jnp.ones((SEG_PER_BLOCK, LRU_BW), F32) for _ in lane_blocks)
        init = (zero, one, zero, one)
    else:
        init = (tuple(h0_ref[0, :, cols] for cols in lane_blocks), (),
                tuple(h0_ref[1, :, cols] for cols in lane_blocks), ())
    lax.fori_loop(0, SEG_LEN, step, init, unroll=8)

    carry_f = carry_b = None
    if chained:
        carry_f, carry_b = [], []
        for n, cols in enumerate(lane_blocks):
            end_h, end_p = u_tm[0, n, last:last + SEG_PER_BLOCK, :], p_tm[0, n, last:last + SEG_PER_BLOCK, :]
            cf = [h0_ref[0, 0:1, cols]]
            for s in range(SEG_PER_BLOCK - 1):
                cf.append(end_h[s:s + 1] + end_p[s:s + 1] * cf[s])
            carry_f.append(cf)
            beg_h, beg_p = u_tm[1, n, 0:SEG_PER_BLOCK, :], p_tm[1, n, 0:SEG_PER_BLOCK, :]
            cb = [None] * SEG_PER_BLOCK
            cb[SEG_PER_BLOCK - 1] = h0_ref[0, 1:2, cols]
            for s in range(SEG_PER_BLOCK - 1, 0, -1):
                cb[s - 1] = beg_h[s:s + 1] + beg_p[s:s + 1] * cb[s]
            carry_b.append(cb)
    else:
        for n, cols in enumerate(lane_blocks):
            fin_ref[0, :, cols] = u_tm[0, n, last:last + SEG_PER_BLOCK, :]
            fin_ref[1, :, cols] = u_tm[1, n, 0:SEG_PER_BLOCK, :]

    for c in range(SEG_PER_BLOCK):
        seg_rows = pl.ds(c, SEG_LEN, stride=SEG_PER_BLOCK)
        tok = slice(c * SEG_LEN, (c + 1) * SEG_LEN)
        for n, cols in enumerate(lane_blocks):
            h = u_tm[0, n, seg_rows, :] + u_tm[1, n, seg_rows, :]
            if chained:
                h = h + p_tm[0, n, seg_rows, :] * carry_f[n][c] + p_tm[1, n, seg_rows, :] * carry_b[n][c]
            ly = lxy_ref[tok, D_LRU + n * LRU_BW:D_LRU + (n + 1) * LRU_BW].astype(F32)
            y_ref[tok, cols] = (h * _gelu_tanh(ly)).astype(BF16)


def _rglru(proj, h0, conv_w, conv_b, w_cat, b_cat, lam, layer, *, chained):
    m = proj.shape[0]
    blocks = m // MIX_ROWS
    lxy_col = 3 * D_ATT // (2 * D_LRU)
    per_layer = lambda *shape: pl.BlockSpec((1,) + shape, lambda b: (layer,) + (0,) * len(shape))
    tm_buf = pltpu.VMEM((2, LRU_BLOCKS, MIX_ROWS, LRU_BW), F32)
    if chained:
        h0_spec = pl.BlockSpec((1, 2, D_LRU), lambda b: (b, 0, 0))
        out_specs = [pl.BlockSpec((MIX_ROWS, D_LRU), lambda b: (b, 0))]
        out_shape = [jax.ShapeDtypeStruct((m, D_LRU), BF16)]
        scratch = [tm_buf, tm_buf, tm_buf]
    else:
        h0_spec = pl.BlockSpec((2, SEG_PER_BLOCK, D_LRU), lambda b: (0, b, 0))
        out_specs = [pl.BlockSpec((MIX_ROWS, D_LRU), lambda b: (b, 0)),
                     pl.BlockSpec((2, SEG_PER_BLOCK, D_LRU), lambda b: (0, b, 0))]
        out_shape = [jax.ShapeDtypeStruct((m, D_LRU), BF16),
                     jax.ShapeDtypeStruct((2, blocks * SEG_PER_BLOCK, D_LRU), F32)]
        scratch = [tm_buf, tm_buf]
    return pl.pallas_call(
        functools.partial(_lru_kernel, chained=chained),
        grid=(blocks,),
        in_specs=[
            pl.BlockSpec((MIX_ROWS, 2 * D_LRU), lambda b: (b, lxy_col)),
            h0_spec,
            per_layer(LRU_CONV, D_LRU), per_layer(1, D_LRU),
            per_layer(LRU_BLOCKS, LRU_BW, 4 * LRU_BW), per_layer(LRU_BLOCKS, 4 * LRU_BW),
            per_layer(2, D_LRU),
        ],
        out_specs=out_specs,
        out_shape=out_shape,
        scratch_shapes=[pltpu.VMEM((MIX_ROWS + 2 * PAD, D_LRU), F32)] + scratch,
        compiler_params=_params("arbitrary"),
        name="rglru",
    )(proj, h0, conv_w, conv_b.reshape(conv_b.shape[0], 1, D_LRU), w_cat, b_cat, lam)


def _sconv_kernel(bc_ref, x_ref, w_ref, y_ref, pad, *, isolated):
    _stage_padded(pad, bc_ref[:, D_SC:2 * D_SC].astype(F32) * x_ref[...].astype(F32))
    w = w_ref[0]
    for c in range(SEG_PER_BLOCK):
        tok = slice(c * SEG_LEN, (c + 1) * SEG_LEN)
        conv = _conv_taps(pad, c, w, SC_CONV // 2, isolated)
        y_ref[tok, :] = (bc_ref[tok, 0:D_SC].astype(F32) * conv).astype(BF16)


def _short_conv(proj, w, layer, *, isolated):
    m = proj.shape[0]
    col0 = 3 * D_ATT + 2 * D_LRU
    return pl.pallas_call(
        functools.partial(_sconv_kernel, isolated=isolated),
        grid=(m // MIX_ROWS,),
        in_specs=[pl.BlockSpec((MIX_ROWS, 2 * D_SC), lambda b: (b, col0 // (2 * D_SC))),
                  pl.BlockSpec((MIX_ROWS, D_SC), lambda b: (b, col0 // D_SC + 2)),
                  pl.BlockSpec((1, SC_CONV, D_SC), lambda b: (layer, 0, 0))],
        out_specs=pl.BlockSpec((MIX_ROWS, D_SC), lambda b: (b, 0)),
        out_shape=jax.ShapeDtypeStruct((m, D_SC), BF16),
        scratch_shapes=[pltpu.VMEM((MIX_ROWS + 2 * PAD, D_SC), F32)],
        compiler_params=_params("arbitrary"),
        name="short_conv",
    )(proj, proj, w)


def _lru_gate_weights(w_a, b_a, w_i, b_i):
    depth = w_a.shape[0]
    w_cat = jnp.concatenate([w_a[:, 0], w_i[:, 0], w_a[:, 1], w_i[:, 1]], axis=-1)
    blocks = lambda b: b.reshape(depth, 2, LRU_BLOCKS, LRU_BW)
    ba, bi = blocks(b_a), blocks(b_i)
    b_cat = jnp.concatenate([ba[:, 0], bi[:, 0], ba[:, 1], bi[:, 1]], axis=-1)
    return 0.5 * w_cat, 0.5 * b_cat


def _stream_layer(h, mod, p, layer, *, rows_per_mod, ctx, caches):
    m = h.shape[0]
    proj = _norm_matmul(h, mod, p["g_mix"], [p["w_in"]], layer, shift_idx=0, scale_idx=1,
                        rows_per_mod=rows_per_mod, tn=512, name="in_proj")
    if ctx is None:
        k_acc, v_acc = caches
        att, k_acc, v_acc = _context_attention(proj, p["q_gain"], p["k_gain"], layer, k_acc, v_acc,
                                               seq=SEG_LEN)
        caches = (k_acc, v_acc)
        h0 = jnp.zeros((2, m // SEG_LEN, D_LRU), F32)
        lru, lru_final = _rglru(proj, h0, p["lru_conv_w"], p["lru_conv_b"], p["w_cat"], p["b_cat"],
                                p["lru_lambda"], layer, chained=False)
    else:
        k_ctx, v_ctx, h0, bias = ctx
        qn, kn = _qk_norm(proj, p["q_gain"], p["k_gain"], layer)
        att = _neighbourhood_attention(qn, kn, proj, k_ctx, v_ctx, bias, layer, seq=MIX_ROWS)
        (lru,) = _rglru(proj, h0, p["lru_conv_w"], p["lru_conv_b"], p["w_cat"], p["b_cat"],
                        p["lru_lambda"], layer, chained=True)
        lru_final = None
    conv = _short_conv(proj, p["sc_conv_w"], layer, isolated=ctx is None)
    h = _matmul_residual([att, lru, conv], p["w_out"], layer, h, mod, gate_idx=2,
                         rows_per_mod=rows_per_mod, tm=m, tn=256, ca=1024, name="out_proj")
    hid = _norm_matmul(h, mod, p["g_ffn"], [p["w_ffn_gate"], p["w_ffn_up"]], layer, shift_idx=3,
                       scale_idx=4, rows_per_mod=rows_per_mod, tn=256, name="ffn_up")
    h = _matmul_residual([hid], p["w_ffn_down"], layer, h, mod, gate_idx=5,
                         rows_per_mod=rows_per_mod, tm=m // 2, tn=256, ca=256, name="ffn_down")
    return h, caches, lru_final


def kernel(x_prompt, x_sample, cache_k, cache_v, state_lru, c, c_ctx, w_mod, b_mod, g_mix, g_ffn, w_in, q_gain, k_gain, na_bias, lru_conv_w, lru_conv_b, lru_w_a, lru_b_a, lru_w_i, lru_b_i, lru_lambda, sc_conv_w, w_out, w_ffn_gate, w_ffn_up, w_ffn_down):
    batch, seq, d = x_prompt.shape
    dec_batch, dec_seq, _ = x_sample.shape
    depth = w_mod.shape[0]
    past = cache_k.shape[2]
    assert seq == SEG_LEN and dec_seq == MIX_ROWS and batch % SEG_PER_BLOCK == 0

    cond = jnp.zeros((MOD_ROWS, d), F32).at[0].set(c_ctx).at[1:1 + dec_batch].set(c)
    mod_all = _modulation(cond, w_mod, b_mod).reshape(depth, MOD_ROWS, 6, d)
    bias_all = _na_bias(na_bias)
    w_cat, b_cat = _lru_gate_weights(lru_w_a, lru_b_a, lru_w_i, lru_b_i)
    p = {
        "g_mix": g_mix, "g_ffn": g_ffn, "w_in": w_in, "q_gain": q_gain, "k_gain": k_gain,
        "lru_conv_w": lru_conv_w, "lru_conv_b": lru_conv_b, "w_cat": w_cat, "b_cat": b_cat,
        "lru_lambda": lru_lambda, "sc_conv_w": sc_conv_w, "w_out": w_out,
        "w_ffn_gate": w_ffn_gate, "w_ffn_up": w_ffn_up, "w_ffn_down": w_ffn_down,
    }

    k_ctx = cache_k.reshape(dec_batch, depth, past, D_ATT)
    v_ctx = cache_v.reshape(dec_batch, depth, past, D_ATT)

    hp = x_prompt.reshape(batch * seq, d)
    hs = x_sample.reshape(dec_batch * dec_seq, d)
    caches = (jnp.zeros((batch, depth, seq, D_ATT), F32), jnp.zeros((batch, depth, seq, D_ATT), F32))
    new_s = []
    for l in range(depth):
        hp, caches, s_l = _stream_layer(hp, mod_all[l, 0:1], p, l, rows_per_mod=batch * seq,
                                        ctx=None, caches=caches)
        new_s.append(jnp.swapaxes(s_l, 0, 1))
        ctx = (k_ctx, v_ctx, state_lru[:, l], bias_all)
        hs, _, _ = _stream_layer(hs, mod_all[l, 1:1 + dec_batch], p, l, rows_per_mod=dec_seq,
                                 ctx=ctx, caches=None)
    cache_shape = (batch, depth, seq, N_HEADS, HEAD_DIM)
    return (hp.reshape(batch, seq, d), hs.reshape(dec_batch, dec_seq, d),
            caches[0].reshape(cache_shape), caches[1].reshape(cache_shape), jnp.stack(new_s, axis=1))
```

```python
import functools

import jax
import jax.numpy as jnp
from jax import lax
from jax.experimental import pallas as pl
from jax.experimental.pallas import tpu as pltpu

D_MODEL = 2048
N_HEADS = 8
HEAD_DIM = 128
D_ATT = N_HEADS * HEAD_DIM
D_LRU = 512
LRU_BLOCKS = 4
LRU_BW = 128
LRU_CONV = 4
LRU_C = 8.0
D_SC = 512
SC_CONV = 3
GRID_W = 64
NA_ROWS = 8
NA_COLS = 16
EPS = 1e-6
LOG2_E = 1.4426950408889634
LOGIT_SCALE = HEAD_DIM ** -0.5 * LOG2_E

NA_QROWS = 4
NA_KROWS = 12
NA_QBLK = NA_QROWS * GRID_W
NA_KBLK = NA_KROWS * GRID_W
MASK_VALUE = -1e30

LANES = 128
SUBLANES = 8
MOD_ROWS = SUBLANES
VMEM_LIMIT_BYTES = 60 * 1024 * 1024

SEG_LEN = 256
SEG_PER_BLOCK = SUBLANES
MIX_ROWS = SEG_LEN * SEG_PER_BLOCK
PAD = SUBLANES
MM_ROWS = 1024

F32 = jnp.float32
BF16 = jnp.bfloat16
F32_MIN_NORMAL = 1.1754943508222875e-38


def _params(*semantics):
    return pltpu.CompilerParams(dimension_semantics=semantics,
                                vmem_limit_bytes=VMEM_LIMIT_BYTES)


def _dot(a, b):
    return jnp.dot(a, b, preferred_element_type=F32)


def _dot_nt(a, b):
    return lax.dot_general(a, b, (((1,), (1,)), ((), ())), preferred_element_type=F32)


def _sigmoid(x):
    return 0.5 * (1.0 + jnp.tanh(0.5 * x))


def _mod_kernel(c_ref, w_ref, b_ref, o_ref):
    c = c_ref[...]
    s = (c * _sigmoid(c)).astype(BF16)
    o_ref[0] = _dot(s, w_ref[0].astype(BF16)) + b_ref[0]


def _modulation(cond, w_mod, b_mod, *, tn=1024):
    depth, d, n = w_mod.shape
    return pl.pallas_call(
        _mod_kernel,
        grid=(depth, n // tn),
        in_specs=[
            pl.BlockSpec((MOD_ROWS, d), lambda l, j: (0, 0)),
            pl.BlockSpec((1, d, tn), lambda l, j: (l, 0, j)),
            pl.BlockSpec((1, 1, tn), lambda l, j: (l, 0, j)),
        ],
        out_specs=pl.BlockSpec((1, MOD_ROWS, tn), lambda l, j: (l, 0, j)),
        out_shape=jax.ShapeDtypeStruct((depth, MOD_ROWS, n), F32),
        compiler_params=_params("arbitrary", "arbitrary"),
        name="modulation",
    )(cond, w_mod, b_mod.reshape(depth, 1, n))


def _norm_mm_kernel(x_ref, mod_ref, g_ref, *rest, n_w, shift_idx, scale_idx, n_chunks):
    w_refs, o_ref, u_ref, wb_refs = rest[:n_w], rest[n_w], rest[n_w + 1], rest[n_w + 2:]
    s = pl.program_id(0)
    cm = x_ref.shape[0]
    slot = jnp.maximum(s - n_chunks, 0) % 2

    def cast_weights(into):
        for w_ref, wb_ref in zip(w_refs, wb_refs):
            wb_ref[into] = w_ref[0].astype(BF16)

    def matmul_rows(row0, rows, use):
        sl = pl.ds(pl.multiple_of(row0, rows), rows)
        u = u_ref[sl, :]
        a = _dot(u, wb_refs[0][use])
        if n_w == 2:
            a = a * _sigmoid(a) * _dot(u, wb_refs[1][use])
        o_ref[sl, :] = a.astype(o_ref.dtype)

    def norm_chunk():
        gain = g_ref[0] * (1.0 + mod_ref[0, scale_idx:scale_idx + 1, :])
        shift = mod_ref[0, shift_idx:shift_idx + 1, :]
        for p in range(cm // LANES):
            x = x_ref[p * LANES:(p + 1) * LANES, :]
            y = x * lax.rsqrt(jnp.mean(x * x, axis=-1, keepdims=True) + EPS)
            u_ref[pl.ds(pl.multiple_of(s * cm + p * LANES, LANES), LANES), :] = (y * gain + shift).astype(BF16)

    @pl.when(s == 0)
    def _():
        cast_weights(0)
        norm_chunk()

    @pl.when((s >= 1) & (s < n_chunks))
    def _():
        matmul_rows((s - 1) * cm, cm, 0)
        norm_chunk()

    @pl.when(s == n_chunks)
    def _():
        matmul_rows((n_chunks - 1) * cm, cm, 0)
        cast_weights(1)

    @pl.when(s > n_chunks)
    def _():
        for r in range(u_ref.shape[0] // MM_ROWS):
            matmul_rows(r * MM_ROWS, MM_ROWS, slot)
        cast_weights(1 - slot)


def _norm_matmul(x, mod, gain, weights, layer, *, shift_idx, scale_idx, rows_per_mod, tn, name,
                 cm=512):
    m, d = x.shape
    n = weights[0].shape[2]
    n_chunks = m // cm
    n_tiles = n // tn
    chunks_per_mod = rows_per_mod // cm
    chunk = lambda s: jnp.minimum(s, n_chunks - 1)
    tile = lambda s: jnp.maximum(s - n_chunks, 0)
    next_tile = lambda s: jnp.minimum(tile(s + 1), n_tiles - 1)
    kernel = functools.partial(_norm_mm_kernel, n_w=len(weights), shift_idx=shift_idx,
                               scale_idx=scale_idx, n_chunks=n_chunks)
    return pl.pallas_call(
        kernel,
        grid=(n_chunks + n_tiles,),
        in_specs=[
            pl.BlockSpec((cm, d), lambda s: (chunk(s), 0)),
            pl.BlockSpec((1, 6, d), lambda s: (chunk(s) // chunks_per_mod, 0, 0)),
            pl.BlockSpec((1, 1, d), lambda s: (layer, 0, 0)),
        ] + [pl.BlockSpec((1, d, tn), lambda s: (layer, 0, next_tile(s))) for _ in weights],
        out_specs=pl.BlockSpec((m, tn), lambda s: (0, tile(s))),
        out_shape=jax.ShapeDtypeStruct((m, n), BF16),
        scratch_shapes=[pltpu.VMEM((m, d), BF16)] + [pltpu.VMEM((2, d, tn), BF16) for _ in weights],
        compiler_params=_params("arbitrary"),
        name=name,
    )(x, mod, gain.reshape(gain.shape[0], 1, d), *weights)


def _mm_res_kernel(*refs, splits, gate_idx, rows_per_mod, n_chunks):
    n = len(splits)
    a_refs = refs[:n]
    w_ref, h_ref, mod_ref, o_ref, a_res, wb_ref = refs[n:]
    s = pl.program_id(1)
    tm = h_ref.shape[0]
    ca = a_refs[0].shape[0]
    tile_row0 = pl.program_id(0) * tm
    slot = jnp.maximum(s - n_chunks + 1, 0) % 2

    def cast_weights(into):
        wb_ref[into] = w_ref[0].astype(BF16)

    def product(row0, rows, use):
        sl = pl.ds(pl.multiple_of(row0, rows), rows)
        gate = mod_ref[(tile_row0 + row0) // rows_per_mod, gate_idx:gate_idx + 1, :]
        o_ref[sl, :] = h_ref[sl, :] + gate * _dot(a_res[sl, :], wb_ref[use])

    def stage_chunk():
        rows = pl.ds(pl.multiple_of(s * ca, ca), ca)
        k0 = 0
        for a_ref, kk in zip(a_refs, splits):
            a_res[rows, k0:k0 + kk] = a_ref[...]
            k0 += kk
        product(s * ca, ca, 0)

    @pl.when(s == 0)
    def _():
        cast_weights(0)
        stage_chunk()

    @pl.when((s >= 1) & (s < n_chunks - 1))
    def _():
        stage_chunk()

    @pl.when(s == n_chunks - 1)
    def _():
        stage_chunk()
        cast_weights(1)

    @pl.when(s >= n_chunks)
    def _():
        for r in range(tm // MM_ROWS):
            product(r * MM_ROWS, MM_ROWS, slot)
        cast_weights(1 - slot)


def _matmul_residual(a_parts, w, layer, h, mod, *, gate_idx, rows_per_mod, tm, tn, ca, name):
    m, n = h.shape
    k = w.shape[1]
    splits = tuple(a.shape[1] for a in a_parts)
    n_chunks = tm // ca
    n_tiles = n // tn
    assert n_chunks >= 2
    chunk = lambda i, s: i * n_chunks + jnp.minimum(s, n_chunks - 1)
    tile = lambda s: jnp.maximum(s - n_chunks + 1, 0)
    next_tile = lambda s: jnp.minimum(tile(s + 1), n_tiles - 1)
    kernel = functools.partial(_mm_res_kernel, splits=splits, gate_idx=gate_idx,
                               rows_per_mod=rows_per_mod, n_chunks=n_chunks)
    return pl.pallas_call(
        kernel,
        grid=(m // tm, n_chunks + n_tiles - 1),
        in_specs=[pl.BlockSpec((ca, kk), lambda i, s: (chunk(i, s), 0)) for kk in splits] + [
            pl.BlockSpec((1, k, tn), lambda i, s: (layer, 0, next_tile(s))),
            pl.BlockSpec((tm, tn), lambda i, s: (i, tile(s))),
            pl.BlockSpec((mod.shape[0], 6, tn), lambda i, s: (0, 0, tile(s))),
        ],
        out_specs=pl.BlockSpec((tm, tn), lambda i, s: (i, tile(s))),
        out_shape=jax.ShapeDtypeStruct((m, n), F32),
        scratch_shapes=[pltpu.VMEM((tm, k), BF16), pltpu.VMEM((2, k, tn), BF16)],
        compiler_params=_params("arbitrary", "arbitrary"),
        name=name,
    )(*a_parts, w, h, mod)


def _head_rms(x, gain):
    return x * lax.rsqrt(jnp.mean(x * x, axis=-1, keepdims=True) + EPS) * gain


def _ctx_attn_kernel(qkv_ref, qg_ref, kg_ref, kacc_ref, vacc_ref, att_ref, kout_ref, vout_ref):
    del kacc_ref, vacc_ref
    qg = qg_ref[0]
    kg = kg_ref[0]
    n_seq, _, seq, _ = kout_ref.shape
    for j in range(n_seq):
        rows = slice(j * seq, (j + 1) * seq)
        for h in range(N_HEADS):
            sl = slice(h * HEAD_DIM, (h + 1) * HEAD_DIM)
            qn = _head_rms(qkv_ref[rows, sl].astype(F32), qg)
            kn = _head_rms(qkv_ref[rows, D_ATT + h * HEAD_DIM:D_ATT + (h + 1) * HEAD_DIM].astype(F32), kg)
            v = qkv_ref[rows, 2 * D_ATT + h * HEAD_DIM:2 * D_ATT + (h + 1) * HEAD_DIM]
            kout_ref[j, 0, :, sl] = kn
            vout_ref[j, 0, :, sl] = v.astype(F32)
            s = _dot_nt(qn.astype(BF16), kn.astype(BF16)) * LOGIT_SCALE
            e = jnp.exp2(s - jnp.max(s, axis=-1, keepdims=True))
            den = jnp.sum(e, axis=-1, keepdims=True)
            att_ref[rows, sl] = (_dot(e.astype(BF16), v) / den).astype(BF16)


def _context_attention(proj, q_gain, k_gain, layer, k_acc, v_acc, *, seq, n_seq=1):
    m = proj.shape[0]
    gain = pl.BlockSpec((1, 1, HEAD_DIM), lambda b: (layer, 0, 0))
    cache = pl.BlockSpec((n_seq, 1, seq, D_ATT), lambda b: (b, layer, 0, 0))
    hbm = pl.BlockSpec(memory_space=pl.ANY)
    gains = lambda g: g.reshape(g.shape[0], 1, HEAD_DIM)
    return pl.pallas_call(
        _ctx_attn_kernel,
        grid=(m // (n_seq * seq),),
        in_specs=[pl.BlockSpec((n_seq * seq, 3 * D_ATT), lambda b: (b, 0)), gain, gain, hbm, hbm],
        out_specs=[pl.BlockSpec((n_seq * seq, D_ATT), lambda b: (b, 0)), cache, cache],
        out_shape=[jax.ShapeDtypeStruct((m, D_ATT), BF16),
                   jax.ShapeDtypeStruct(k_acc.shape, F32),
                   jax.ShapeDtypeStruct(v_acc.shape, F32)],
        input_output_aliases={3: 1, 4: 2},
        compiler_params=_params("arbitrary"),
        name="context_attention",
    )(proj, gains(q_gain), gains(k_gain), k_acc, v_acc)


def _qk_norm_kernel(qk_ref, qg_ref, kg_ref, qo_ref, ko_ref):
    qg = qg_ref[0]
    kg = kg_ref[0]
    for h in range(N_HEADS):
        sl = slice(h * HEAD_DIM, (h + 1) * HEAD_DIM)
        qo_ref[:, sl] = _head_rms(qk_ref[:, sl].astype(F32), qg).astype(BF16)
        k = qk_ref[:, D_ATT + h * HEAD_DIM:D_ATT + (h + 1) * HEAD_DIM]
        ko_ref[:, sl] = _head_rms(k.astype(F32), kg).astype(BF16)


def _qk_norm(proj, q_gain, k_gain, layer, *, tm=1024):
    m = proj.shape[0]
    gain = pl.BlockSpec((1, 1, HEAD_DIM), lambda i: (layer, 0, 0))
    out = pl.BlockSpec((tm, D_ATT), lambda i: (i, 0))
    gains = lambda g: g.reshape(g.shape[0], 1, HEAD_DIM)
    return pl.pallas_call(
        _qk_norm_kernel,
        grid=(m // tm,),
        in_specs=[pl.BlockSpec((tm, 2 * D_ATT), lambda i: (i, 0)), gain, gain],
        out_specs=[out, out],
        out_shape=[jax.ShapeDtypeStruct((m, D_ATT), BF16)] * 2,
        compiler_params=_params("arbitrary"),
        name="qk_norm",
    )(proj, gains(q_gain), gains(k_gain))


def _na_block_geometry():
    rows = 32
    return ((0, 0), (NA_QROWS, 0), (rows - NA_QROWS, rows - NA_KROWS)), rows


def _na_bias_kernel(rel_ref, o_ref):
    qc = lax.broadcasted_iota(jnp.int32, (GRID_W, GRID_W), 0)
    kc = lax.broadcasted_iota(jnp.int32, (GRID_W, GRID_W), 1)
    cs = jnp.clip(qc - NA_COLS // 2, 0, GRID_W - NA_COLS)
    col_ok = (kc >= cs) & (kc < cs + NA_COLS)
    dc = kc - qc + (NA_COLS - 1)
    hits = [dc == j for j in range(2 * NA_COLS - 1)]
    masked = jnp.full((GRID_W, GRID_W), MASK_VALUE, F32)
    tiles = []
    for dr in range(2 * NA_ROWS - 1):
        x = masked
        for j, hit in enumerate(hits):
            x = jnp.where(hit, rel_ref[0, 0, dr, j] * LOG2_E, x)
        tiles.append(jnp.where(col_ok, x, MASK_VALUE))
    kinds, rows = _na_block_geometry()
    for kind, (r0, kb) in enumerate(kinds):
        for qr in range(NA_QROWS):
            r = r0 + qr
            rs = min(max(r - NA_ROWS // 2, 0), rows - NA_ROWS)
            for kr in range(NA_KROWS):
                krow = kb + kr
                tile = tiles[krow - r + NA_ROWS - 1] if rs <= krow < rs + NA_ROWS else masked
                o_ref[0, kind, 0, qr * GRID_W:(qr + 1) * GRID_W,
                      kr * GRID_W:(kr + 1) * GRID_W] = tile


def _na_bias(na_bias):
    depth = na_bias.shape[0]
    return pl.pallas_call(
        _na_bias_kernel,
        grid=(depth, N_HEADS),
        in_specs=[pl.BlockSpec((1, 1) + na_bias.shape[2:], lambda l, h: (l, h, 0, 0),
                               memory_space=pltpu.SMEM)],
        out_specs=pl.BlockSpec((1, 3, 1, NA_QBLK, NA_KBLK), lambda l, h: (l, 0, h, 0, 0)),
        out_shape=jax.ShapeDtypeStruct((depth, 3, N_HEADS, NA_QBLK, NA_KBLK), F32),
        compiler_params=_params("arbitrary", "arbitrary"),
        name="na_bias",
    )(na_bias)


def _na_attn_kernel(q_ref, k_ref, v_ref, kc_ref, vc_ref, bias_ref, o_ref, *, n_blocks):
    blk = pl.program_id(1)
    start = pl.multiple_of(jnp.clip(blk - 1, 0, n_blocks - 3) * NA_QBLK, NA_QBLK)
    for h in range(N_HEADS):
        sl = slice(h * HEAD_DIM, (h + 1) * HEAD_DIM)
        q = q_ref[:, sl]
        kw = k_ref[pl.ds(start, NA_KBLK), sl]
        vw = v_ref[pl.ds(start, NA_KBLK), sl]
        s_loc = _dot_nt(q, kw) * LOGIT_SCALE + bias_ref[0, 0, h]
        s_ctx = _dot_nt(q, kc_ref[0, 0, :, sl].astype(BF16)) * LOGIT_SCALE
        mx = jnp.maximum(jnp.max(s_loc, axis=-1, keepdims=True),
                         jnp.max(s_ctx, axis=-1, keepdims=True))
        e_loc = jnp.exp2(s_loc - mx)
        e_ctx = jnp.exp2(s_ctx - mx)
        den = jnp.sum(e_loc, axis=-1, keepdims=True) + jnp.sum(e_ctx, axis=-1, keepdims=True)
        o = _dot(e_loc.astype(BF16), vw) + _dot(e_ctx.astype(BF16), vc_ref[0, 0, :, sl].astype(BF16))
        o_ref[:, sl] = (o / den).astype(BF16)


def _neighbourhood_attention(qn, kn, proj, k_ctx, v_ctx, bias, layer, *, seq):
    m = qn.shape[0]
    batch = m // seq
    n_blocks = seq // NA_QBLK
    past = k_ctx.shape[2]
    kind = lambda i: jnp.where(i == 0, 0, jnp.where(i == n_blocks - 1, 2, 1))
    ctx = pl.BlockSpec((1, 1, past, D_ATT), lambda b, i: (b, layer, 0, 0))
    return pl.pallas_call(
        functools.partial(_na_attn_kernel, n_blocks=n_blocks),
        grid=(batch, n_blocks),
        in_specs=[
            pl.BlockSpec((NA_QBLK, D_ATT), lambda b, i: (b * n_blocks + i, 0)),
            pl.BlockSpec((seq, D_ATT), lambda b, i: (b, 0)),
            pl.BlockSpec((seq, D_ATT), lambda b, i: (b, 2)),
            ctx, ctx,
            pl.BlockSpec((1, 1, N_HEADS, NA_QBLK, NA_KBLK), lambda b, i: (layer, kind(i), 0, 0, 0)),
        ],
        out_specs=pl.BlockSpec((NA_QBLK, D_ATT), lambda b, i: (b * n_blocks + i, 0)),
        out_shape=jax.ShapeDtypeStruct((m, D_ATT), BF16),
        compiler_params=_params("arbitrary", "arbitrary"),
        name="neighbourhood_attention",
    )(qn, kn, proj, k_ctx, v_ctx, bias)


def _stage_padded(pad_ref, x):
    zeros = jnp.zeros((PAD, pad_ref.shape[1]), F32)
    rows = x.shape[0]
    pad_ref[0:PAD, :] = zeros
    pad_ref[PAD + rows:2 * PAD + rows, :] = zeros
    pad_ref[PAD:PAD + rows, :] = x


def _conv_taps(pad_ref, seg, w, left, isolated):
    base = seg * SEG_LEN
    t = lax.broadcasted_iota(jnp.int32, (SEG_LEN, 1), 0)
    acc = None
    for k in range(w.shape[0]):
        off = k - left
        xs = pad_ref[PAD + base + off:PAD + base + off + SEG_LEN, :]
        if isolated and off != 0:
            xs = jnp.where((t + off >= 0) & (t + off < SEG_LEN), xs, 0.0)
        term = xs * w[k:k + 1, :]
        acc = term if acc is None else acc + term
    return acc


def _gelu_tanh(x):
    return x * (0.5 * (1.0 + jnp.tanh(0.7978845608028654 * (x + 0.044715 * (x * x * x)))))


def _lru_kernel(lxy_ref, h0_ref, cw_ref, cb_ref, w_ref, b_ref, lam_ref, *rest, chained):
    if chained:
        y_ref, xpad, a_tm, u_tm, p_tm = rest
    else:
        y_ref, fin_ref, xpad, a_tm, u_tm = rest
    last = (SEG_LEN - 1) * SEG_PER_BLOCK
    _stage_padded(xpad, lxy_ref[:, 0:D_LRU].astype(F32))
    cw = cw_ref[0]
    neg_lam = -lam_ref[0]
    softplus = jnp.maximum(neg_lam, 0.0) + jnp.log1p(jnp.exp(-jnp.abs(neg_lam)))
    decay = (-0.5 * LRU_C) * softplus
    for c in range(SEG_PER_BLOCK):
        xc = cb_ref[0] + _conv_taps(xpad, c, cw, LRU_CONV // 2, not chained)
        seg_rows = pl.ds(c, SEG_LEN, stride=SEG_PER_BLOCK)
        for n in range(LRU_BLOCKS):
            cols = slice(n * LRU_BW, (n + 1) * LRU_BW)
            xb = xc[:, cols]
            g = _dot(xb.astype(BF16), w_ref[0, n].astype(BF16)) + b_ref[0, n:n + 1, :]
            x_half = 0.5 * xb
            for d in range(2):
                ta = jnp.tanh(g[:, (2 * d) * LRU_BW:(2 * d + 1) * LRU_BW])
                ti = jnp.tanh(g[:, (2 * d + 1) * LRU_BW:(2 * d + 2) * LRU_BW])
                log_a = decay[d:d + 1, cols] * (1.0 + ta)
                th = jnp.tanh(log_a)
                p = -2.0 * th
                coef = p * lax.rsqrt(jnp.maximum(p * (1.0 - th), F32_MIN_NORMAL))
                a_tm[d, n, seg_rows, :] = jnp.exp(log_a)
                u_tm[d, n, seg_rows, :] = coef * ((1.0 + ti) * x_half)

    def scan_dir(d, rows, h, p):
        h_new, p_new = [], []
        for n in range(LRU_BLOCKS):
            a = a_tm[d, n, rows, :]
            hn = a * h[n] + u_tm[d, n, rows, :]
            u_tm[d, n, rows, :] = hn
            h_new.append(hn)
            if chained:
                pn = a * p[n]
                p_tm[d, n, rows, :] = pn
                p_new.append(pn)
        return tuple(h_new), tuple(p_new)

    def step(t, carry):
        hf, pf, hb, pb = carry
        hf, pf = scan_dir(0, pl.ds(pl.multiple_of(t * SEG_PER_BLOCK, SEG_PER_BLOCK), SEG_PER_BLOCK), hf, pf)
        tb = SEG_LEN - 1 - t
        hb, pb = scan_dir(1, pl.ds(pl.multiple_of(tb * SEG_PER_BLOCK, SEG_PER_BLOCK), SEG_PER_BLOCK), hb, pb)
        return hf, pf, hb, pb

    lane_blocks = [slice(n * LRU_BW, (n + 1) * LRU_BW) for n in range(LRU_BLOCKS)]
    if chained:
        zero = tuple(jnp.zeros((SEG_PER_BLOCK, LRU_BW), F32) for _ in lane_blocks)
        one = tuple(jnp.ones((SEG_PER_BLOCK, LRU_BW), F32) for _ in lane_blocks)
        init = (zero, one, zero, one)
    else:
        init = (tuple(h0_ref[0, :, cols] for cols in lane_blocks), (),
                tuple(h0_ref[1, :, cols] for cols in lane_blocks), ())
    lax.fori_loop(0, SEG_LEN, step, init, unroll=8)

    carry_f = carry_b = None
    if chained:
        carry_f, carry_b = [], []
        for n, cols in enumerate(lane_blocks):
            end_h, end_p = u_tm[0, n, last:last + SEG_PER_BLOCK, :], p_tm[0, n, last:last + SEG_PER_BLOCK, :]
            cf = [h0_ref[0, 0:1, cols]]
            for s in range(SEG_PER_BLOCK - 1):
                cf.append(end_h[s:s + 1] + end_p[s:s + 1] * cf[s])
            carry_f.append(cf)
            beg_h, beg_p = u_tm[1, n, 0:SEG_PER_BLOCK, :], p_tm[1, n, 0:SEG_PER_BLOCK, :]
            cb = [None] * SEG_PER_BLOCK
            cb[SEG_PER_BLOCK - 1] = h0_ref[0, 1:2, cols]
            for s in range(SEG_PER_BLOCK - 1, 0, -1):
                cb[s - 1] = beg_h[s:s + 1] + beg_p[s:s + 1] * cb[s]
            carry_b.append(cb)
    else:
        for n, cols in enumerate(lane_blocks):
            fin_ref[0, :, cols] = u_tm[0, n, last:last + SEG_PER_BLOCK, :]
            fin_ref[1, :, cols] = u_tm[1, n, 0:SEG_PER_BLOCK, :]

    for c in range(SEG_PER_BLOCK):
        seg_rows = pl.ds(c, SEG_LEN, stride=SEG_PER_BLOCK)
        tok = slice(c * SEG_LEN, (c + 1) * SEG_LEN)
        for n, cols in enumerate(lane_blocks):
            h = u_tm[0, n, seg_rows, :] + u_tm[1, n, seg_rows, :]
            if chained:
                h = h + p_tm[0, n, seg_rows, :] * carry_f[n][c] + p_tm[1, n, seg_rows, :] * carry_b[n][c]
            ly = lxy_ref[tok, D_LRU + n * LRU_BW:D_LRU + (n + 1) * LRU_BW].astype(F32)
            y_ref[tok, cols] = (h * _gelu_tanh(ly)).astype(BF16)


def _rglru(proj, h0, conv_w, conv_b, w_cat, b_cat, lam, layer, *, chained):
    m = proj.shape[0]
    blocks = m // MIX_ROWS
    lxy_col = 3 * D_ATT // (2 * D_LRU)
    per_layer = lambda *shape: pl.BlockSpec((1,) + shape, lambda b: (layer,) + (0,) * len(shape))
    tm_buf = pltpu.VMEM((2, LRU_BLOCKS, MIX_ROWS, LRU_BW), F32)
    if chained:
        h0_spec = pl.BlockSpec((1, 2, D_LRU), lambda b: (b, 0, 0))
        out_specs = [pl.BlockSpec((MIX_ROWS, D_LRU), lambda b: (b, 0))]
        out_shape = [jax.ShapeDtypeStruct((m, D_LRU), BF16)]
        scratch = [tm_buf, tm_buf, tm_buf]
    else:
        h0_spec = pl.BlockSpec((2, SEG_PER_BLOCK, D_LRU), lambda b: (0, b, 0))
        out_specs = [pl.BlockSpec((MIX_ROWS, D_LRU), lambda b: (b, 0)),
                     pl.BlockSpec((2, SEG_PER_BLOCK, D_LRU), lambda b: (0, b, 0))]
        out_shape = [jax.ShapeDtypeStruct((m, D_LRU), BF16),
                     jax.ShapeDtypeStruct((2, blocks * SEG_PER_BLOCK, D_LRU), F32)]
        scratch = [tm_buf, tm_buf]
    return pl.pallas_call(
        functools.partial(_lru_kernel, chained=chained),
        grid=(blocks,),
        in_specs=[
            pl.BlockSpec((MIX_ROWS, 2 * D_LRU), lambda b: (b, lxy_col)),
            h0_spec,
            per_layer(LRU_CONV, D_LRU), per_layer(1, D_LRU),
            per_layer(LRU_BLOCKS, LRU_BW, 4 * LRU_BW), per_layer(LRU_BLOCKS, 4 * LRU_BW),
            per_layer(2, D_LRU),
        ],
        out_specs=out_specs,
        out_shape=out_shape,
        scratch_shapes=[pltpu.VMEM((MIX_ROWS + 2 * PAD, D_LRU), F32)] + scratch,
        compiler_params=_params("arbitrary"),
        name="rglru",
    )(proj, h0, conv_w, conv_b.reshape(conv_b.shape[0], 1, D_LRU), w_cat, b_cat, lam)


def _sconv_kernel(bc_ref, x_ref, w_ref, y_ref, pad, *, isolated):
    _stage_padded(pad, bc_ref[:, D_SC:2 * D_SC].astype(F32) * x_ref[...].astype(F32))
    w = w_ref[0]
    for c in range(SEG_PER_BLOCK):
        tok = slice(c * SEG_LEN, (c + 1) * SEG_LEN)
        conv = _conv_taps(pad, c, w, SC_CONV // 2, isolated)
        y_ref[tok, :] = (bc_ref[tok, 0:D_SC].astype(F32) * conv).astype(BF16)


def _short_conv(proj, w, layer, *, isolated):
    m = proj.shape[0]
    col0 = 3 * D_ATT + 2 * D_LRU
    return pl.pallas_call(
        functools.partial(_sconv_kernel, isolated=isolated),
        grid=(m // MIX_ROWS,),
        in_specs=[pl.BlockSpec((MIX_ROWS, 2 * D_SC), lambda b: (b, col0 // (2 * D_SC))),
                  pl.BlockSpec((MIX_ROWS, D_SC), lambda b: (b, col0 // D_SC + 2)),
                  pl.BlockSpec((1, SC_CONV, D_SC), lambda b: (layer, 0, 0))],
        out_specs=pl.BlockSpec((MIX_ROWS, D_SC), lambda b: (b, 0)),
        out_shape=jax.ShapeDtypeStruct((m, D_SC), BF16),
        scratch_shapes=[pltpu.VMEM((MIX_ROWS + 2 * PAD, D_SC), F32)],
        compiler_params=_params("arbitrary"),
        name="short_conv",
    )(proj, proj, w)


def _lru_gate_weights(w_a, b_a, w_i, b_i):
    depth = w_a.shape[0]
    w_cat = jnp.concatenate([w_a[:, 0], w_i[:, 0], w_a[:, 1], w_i[:, 1]], axis=-1)
    blocks = lambda b: b.reshape(depth, 2, LRU_BLOCKS, LRU_BW)
    ba, bi = blocks(b_a), blocks(b_i)
    b_cat = jnp.concatenate([ba[:, 0], bi[:, 0], ba[:, 1], bi[:, 1]], axis=-1)
    return 0.5 * w_cat, 0.5 * b_cat


def _stream_layer(h, mod, p, layer, *, rows_per_mod, ctx, caches):
    m = h.shape[0]
    proj = _norm_matmul(h, mod, p["g_mix"], [p["w_in"]], layer, shift_idx=0, scale_idx=1,
                        rows_per_mod=rows_per_mod, tn=512, name="in_proj")
    if ctx is None:
        k_acc, v_acc = caches
        att, k_acc, v_acc = _context_attention(proj, p["q_gain"], p["k_gain"], layer, k_acc, v_acc,
                                               seq=SEG_LEN)
        caches = (k_acc, v_acc)
        h0 = jnp.zeros((2, m // SEG_LEN, D_LRU), F32)
        lru, lru_final = _rglru(proj, h0, p["lru_conv_w"], p["lru_conv_b"], p["w_cat"], p["b_cat"],
                                p["lru_lambda"], layer, chained=False)
    else:
        k_ctx, v_ctx, h0, bias = ctx
        qn, kn = _qk_norm(proj, p["q_gain"], p["k_gain"], layer)
        att = _neighbourhood_attention(qn, kn, proj, k_ctx, v_ctx, bias, layer, seq=MIX_ROWS)
        (lru,) = _rglru(proj, h0, p["lru_conv_w"], p["lru_conv_b"], p["w_cat"], p["b_cat"],
                        p["lru_lambda"], layer, chained=True)
        lru_final = None
    conv = _short_conv(proj, p["sc_conv_w"], layer, isolated=ctx is None)
    h = _matmul_residual([att, lru, conv], p["w_out"], layer, h, mod, gate_idx=2,
                         rows_per_mod=rows_per_mod, tm=m, tn=256, ca=1024, name="out_proj")
    hid = _norm_matmul(h, mod, p["g_ffn"], [p["w_ffn_gate"], p["w_ffn_up"]], layer, shift_idx=3,
                       scale_idx=4, rows_per_mod=rows_per_mod, tn=256, name="ffn_up")
    h = _matmul_residual([hid], p["w_ffn_down"], layer, h, mod, gate_idx=5,
                         rows_per_mod=rows_per_mod, tm=m // 2, tn=256, ca=256, name="ffn_down")
    return h, caches, lru_final


def kernel(x_prompt, x_sample, cache_k, cache_v, state_lru, c, c_ctx, w_mod, b_mod, g_mix, g_ffn, w_in, q_gain, k_gain, na_bias, lru_conv_w, lru_conv_b, lru_w_a, lru_b_a, lru_w_i, lru_b_i, lru_lambda, sc_conv_w, w_out, w_ffn_gate, w_ffn_up, w_ffn_down):
    batch, seq, d = x_prompt.shape
    dec_batch, dec_seq, _ = x_sample.shape
    depth = w_mod.shape[0]
    past = cache_k.shape[2]
    assert seq == SEG_LEN and dec_seq == MIX_ROWS and batch % SEG_PER_BLOCK == 0

    cond = jnp.zeros((MOD_ROWS, d), F32).at[0].set(c_ctx).at[1:1 + dec_batch].set(c)
    mod_all = _modulation(cond, w_mod, b_mod).reshape(depth, MOD_ROWS, 6, d)
    bias_all = _na_bias(na_bias)
    w_cat, b_cat = _lru_gate_weights(lru_w_a, lru_b_a, lru_w_i, lru_b_i)
    p = {
        "g_mix": g_mix, "g_ffn": g_ffn, "w_in": w_in, "q_gain": q_gain, "k_gain": k_gain,
        "lru_conv_w": lru_conv_w, "lru_conv_b": lru_conv_b, "w_cat": w_cat, "b_cat": b_cat,
        "lru_lambda": lru_lambda, "sc_conv_w": sc_conv_w, "w_out": w_out,
        "w_ffn_gate": w_ffn_gate, "w_ffn_up": w_ffn_up, "w_ffn_down": w_ffn_down,
    }

    k_ctx = cache_k.reshape(dec_batch, depth, past, D_ATT)
    v_ctx = cache_v.reshape(dec_batch, depth, past, D_ATT)

    hp = x_prompt.reshape(batch * seq, d)
    hs = x_sample.reshape(dec_batch * dec_seq, d)
    caches = (jnp.zeros((batch, depth, seq, D_ATT), F32), jnp.zeros((batch, depth, seq, D_ATT), F32))
    new_s = []
    for l in range(depth):
        hp, caches, s_l = _stream_layer(hp, mod_all[l, 0:1], p, l, rows_per_mod=batch * seq,
                                        ctx=None, caches=caches)
        new_s.append(jnp.swapaxes(s_l, 0, 1))
        ctx = (k_ctx, v_ctx, state_lru[:, l], bias_all)
        hs, _, _ = _stream_layer(hs, mod_all[l, 1:1 + dec_batch], p, l, rows_per_mod=dec_seq,
                                 ctx=ctx, caches=None)
    cache_shape = (batch, depth, seq, N_HEADS, HEAD_DIM)
    return (hp.reshape(batch, seq, d), hs.reshape(dec_batch, dec_seq, d),
            caches[0].reshape(cache_shape), caches[1].reshape(cache_shape), jnp.stack(new_s, axis=1))
```
